```python
import jax
import jax.numpy as jnp
from jax import lax
import numpy as np

D_MODEL = 1024
BATCH = 2
SEQ = 8192
DEPTH = 4

CTX_LEN = 256
GRID_W = 64
EPS = 1e-6
F32 = jnp.float32

ATT_HEADS = 8
ATT_KV_HEADS = 2
ATT_GROUP = ATT_HEADS // ATT_KV_HEADS
HEAD_DIM = 64
WINDOW = 128
WBLK = 128
ROPE_BASE = 10000.0
ROPE_AXIS_DIM = HEAD_DIM // 2
ROPE_FREQS = ROPE_AXIS_DIM // 2

GLA_HEADS = 4
GLA_DK = 64
GLA_DV = 128
GLA_RANK = 16
GLA_TAU = 16.0

RET_HEADS = 4
RET_DK = 64
RET_DV = 128

CHUNK = 64
N_BRANCH = 3
BRANCH_W = ATT_HEADS * HEAD_DIM

D_FF = 2816
CONV_W = 3

IN_SIZES = (
    ATT_HEADS * HEAD_DIM, ATT_KV_HEADS * HEAD_DIM, ATT_KV_HEADS * HEAD_DIM,
    GLA_HEADS * GLA_DK, GLA_HEADS * GLA_DK, GLA_HEADS * GLA_DV, GLA_HEADS * GLA_DV,
    2 * GLA_RANK,
    RET_HEADS * RET_DK, RET_HEADS * RET_DK, RET_HEADS * RET_DV, RET_HEADS * RET_DV,
    N_BRANCH * D_MODEL,
)
D_IN = sum(IN_SIZES)

kernel_name = 'hybrid_parallel_mixer_dit'


def heads(a, n):
    return a.reshape(*a.shape[:-1], n, a.shape[-1] // n)


def rms_norm(a, g):
    af = a.astype(F32)
    y = af * lax.rsqrt(jnp.mean(af * af, axis=-1, keepdims=True) + EPS)
    return (y * g.astype(F32)).astype(a.dtype)


def split_in(z):
    idx = [int(i) for i in np.cumsum(IN_SIZES)[:-1]]
    return jnp.split(z, idx, axis=-1)


def axial_rope_tables(n_tokens):
    rows = n_tokens // GRID_W
    row = jnp.repeat(jnp.arange(rows, dtype=F32), GRID_W)
    col = jnp.tile(jnp.arange(GRID_W, dtype=F32), rows)
    inv = ROPE_BASE ** (-jnp.arange(ROPE_FREQS, dtype=F32) * 2.0 / ROPE_AXIS_DIM)
    ang = jnp.stack([row[:, None] * inv, col[:, None] * inv], axis=1)
    return jnp.cos(ang), jnp.sin(ang)


def axial_rope(a, cos, sin):
    af = a.astype(F32).reshape(*a.shape[:-1], 2, 2, ROPE_FREQS)
    a1, a2 = af[..., 0, :], af[..., 1, :]
    c = cos[None, :, None]
    s = sin[None, :, None]
    out = jnp.stack([a1 * c - a2 * s, a2 * c + a1 * s], axis=-2)
    return out.reshape(a.shape).astype(a.dtype)


def window_attention(q, k, v, kc, vc, sink):
    B, T, H, D = q.shape
    L = kc.shape[1]
    nb = T // WBLK
    scale = D ** -0.5
    qb = q.reshape(B, nb, WBLK, ATT_KV_HEADS, ATT_GROUP, D)

    def band(a):
        ap = jnp.pad(a, ((0, 0), (WBLK, WBLK), (0, 0), (0, 0)))
        return jnp.concatenate(
            [ap[:, j * WBLK:j * WBLK + T].reshape(B, nb, WBLK, ATT_KV_HEADS, D) for j in range(3)], axis=2)

    kb, vb = band(k), band(v)
    slot = jnp.arange(3 * WBLK)
    rel = slot[None, :] - WBLK - jnp.arange(WBLK)[:, None]
    kpos = jnp.arange(nb)[:, None] * WBLK - WBLK + slot[None, :]
    mask = (jnp.abs(rel) <= WINDOW)[None] & ((kpos >= 0) & (kpos < T))[:, None, :]

    s_loc = jnp.einsum('bnihgd,bnshd->bnhgis', qb, kb).astype(F32) * scale
    s_loc = jnp.where(mask[None, :, None, None], s_loc, -jnp.inf)
    s_ctx = jnp.einsum('bnihgd,bchd->bnhgic', qb, kc).astype(F32) * scale
    s_sink = jnp.broadcast_to(sink.astype(F32).reshape(1, 1, ATT_KV_HEADS, ATT_GROUP, 1, 1),
                              s_loc.shape[:-1] + (1,))
    p = jax.nn.softmax(jnp.concatenate([s_loc, s_ctx, s_sink], axis=-1), axis=-1).astype(v.dtype)
    o = (jnp.einsum('bnhgis,bnshd->bnihgd', p[..., :3 * WBLK], vb)
         + jnp.einsum('bnhgic,bchd->bnihgd', p[..., 3 * WBLK:3 * WBLK + L], vc))
    return o.reshape(B, T, H * D)


def context_attention(qc, kc, vc, sink):
    B, L, H, D = qc.shape
    qg = qc.reshape(B, L, ATT_KV_HEADS, ATT_GROUP, D)
    s = jnp.einsum('bihgd,bjhd->bhgij', qg, kc).astype(F32) * D ** -0.5
    s_sink = jnp.broadcast_to(sink.astype(F32).reshape(1, ATT_KV_HEADS, ATT_GROUP, 1, 1), s.shape[:-1] + (1,))
    p = jax.nn.softmax(jnp.concatenate([s, s_sink], axis=-1), axis=-1).astype(vc.dtype)
    o = jnp.einsum('bhgij,bjhd->bihgd', p[..., :L], vc)
    return o.reshape(B, L, H * D)


def chunk_scan(q, k, v, g, s0):
    B, T, H, K = q.shape
    V = v.shape[-1]
    n = T // CHUNK
    q, k, v, g = (a.astype(F32).reshape(B, n, CHUNK, H, a.shape[-1]) for a in (q, k, v, g))
    b = jnp.cumsum(g, axis=2)
    b_last = b[:, :, -1]
    q_in = q * jnp.exp(b)
    k_in = k * jnp.exp(-b)
    lower = jnp.tril(jnp.ones((CHUNK, CHUNK), dtype=bool))
    att = jnp.where(lower, jnp.einsum('bnihk,bnjhk->bnhij', q_in, k_in), 0.0)
    o_intra = jnp.einsum('bnhij,bnjhv->bnihv', att, v)
    k_end = k * jnp.exp(b_last[:, :, None] - b)
    ds = jnp.einsum('bnjhk,bnjhv->bnhkv', k_end, v)

    def step(S, inp):
        dec, d = inp
        return S * jnp.exp(dec)[..., None] + d, S

    s_fin, s_start = lax.scan(step, s0.astype(F32), (jnp.moveaxis(b_last, 1, 0), jnp.moveaxis(ds, 1, 0)))
    s_start = jnp.moveaxis(s_start, 0, 1)
    o = o_intra + jnp.einsum('bnihk,bnhkv->bnihv', q_in, s_start)
    return o.reshape(B, T, H, V), s_fin


def bidir_scan(lat, ctx_in):
    q, k, v, gf, gb = lat
    qc, kc, vc, gcf, gcb = ctx_in
    B, _, H, K = q.shape
    zero = jnp.zeros((B, H, K, v.shape[-1]), F32)
    oc_f, sc_f = chunk_scan(qc, kc, vc, gcf, zero)
    oc_b, sc_b = chunk_scan(*(jnp.flip(a, 1) for a in (qc, kc, vc, gcb)), zero)
    o_f, _ = chunk_scan(q, k, v, gf, sc_f)
    o_b, _ = chunk_scan(*(jnp.flip(a, 1) for a in (q, k, v, gb)), sc_b)
    return o_f + jnp.flip(o_b, 1), oc_f + jnp.flip(oc_b, 1)


def conv_ffn(h, w_up, conv_w, conv_b, w_down):
    u = h @ w_up
    u = lax.conv_general_dilated(u, conv_w[:, None, :], window_strides=(1,), padding=((1, 1),),
                                 dimension_numbers=('NWC', 'WIO', 'NWC'),
                                 feature_group_count=u.shape[-1]) + conv_b
    a, bv = jnp.split(u, 2, axis=-1)
    return (jax.nn.silu(a) * bv) @ w_down


def token_mixers(h, hc, w_in, q_norm_g, k_norm_g, sink, gla_gate_w, gla_gate_b, gla_norm_g,
                 ret_log_decay, ret_norm_g, w_branch, w_out, cos, sin, with_ctx):
    (aq, ak, av, gq, gk, gv, gr, ga, rq, rk, rv, rg, mg) = split_in(h @ w_in)
    (caq, cak, cav, cgq, cgk, cgv, cgr, cga, crq, crk, crv, crg, cmg) = split_in(hc @ w_in)

    qa = axial_rope(rms_norm(heads(aq, ATT_HEADS), q_norm_g), cos, sin)
    ka = axial_rope(rms_norm(heads(ak, ATT_KV_HEADS), k_norm_g), cos, sin)
    kca = rms_norm(heads(cak, ATT_KV_HEADS), k_norm_g)
    vca = heads(cav, ATT_KV_HEADS)
    o_a = window_attention(qa, ka, heads(av, ATT_KV_HEADS), kca, vca, sink)

    def gla_inputs(q, k, v, a):
        lr = heads(a, 2)
        log_a = jax.nn.log_sigmoid(
            (jnp.einsum('btzr,zrk->btzk', lr, gla_gate_w) + gla_gate_b).astype(F32)) / GLA_TAU
        return (heads(q, GLA_HEADS) * GLA_DK ** -0.5, heads(k, GLA_HEADS), heads(v, GLA_HEADS),
                heads(log_a[:, :, 0], GLA_HEADS), heads(log_a[:, :, 1], GLA_HEADS))

    o_g, oc_g = bidir_scan(gla_inputs(gq, gk, gv, ga), gla_inputs(cgq, cgk, cgv, cga))

    log_gamma = -jnp.exp(ret_log_decay.astype(F32))

    def ret_inputs(q, k, v, rotate):
        q = heads(q, RET_HEADS)
        k = heads(k, RET_HEADS) * RET_DK ** -0.5
        if rotate:
            q, k = axial_rope(q, cos, sin), axial_rope(k, cos, sin)
        return (q, k, heads(v, RET_HEADS),
                jnp.broadcast_to(log_gamma[0][:, None], q.shape),
                jnp.broadcast_to(log_gamma[1][:, None], q.shape))

    o_r, oc_r = bidir_scan(ret_inputs(rq, rk, rv, True), ret_inputs(crq, crk, crv, False))

    def gated_out(o, g, norm_g, n):
        y = rms_norm(o, norm_g) * jax.nn.silu(heads(g, n)).astype(F32)
        return y.reshape(g.shape).astype(g.dtype)

    def merge(oa, og, orr, m):
        ys = jnp.einsum('btzw,zwd->btzd', jnp.stack([oa, og, orr], axis=2), w_branch)
        return jnp.sum(jax.nn.sigmoid(heads(m, N_BRANCH)) * ys, axis=2) @ w_out

    y = merge(o_a, gated_out(o_g, gr, gla_norm_g, GLA_HEADS), gated_out(o_r, rg, ret_norm_g, RET_HEADS), mg)
    if not with_ctx:
        return y, None
    qca = rms_norm(heads(caq, ATT_HEADS), q_norm_g)
    yc = merge(context_attention(qca, kca, vca, sink),
               gated_out(oc_g, cgr, gla_norm_g, GLA_HEADS),
               gated_out(oc_r, crg, ret_norm_g, RET_HEADS), cmg)
    return y, yc


def setup_inputs(seed: int = 0) -> dict:
    key = jax.random.key(seed)
    ks = jax.random.split(key, 24)
    D = D_MODEL

    def nrm(k, shape, scale):
        return jax.random.normal(k, shape, F32) * scale

    ret_base = jnp.log(-jnp.log(1.0 - 2.0 ** (-5.0 - jnp.arange(RET_HEADS, dtype=F32))))
    return {
        'x': nrm(ks[0], (BATCH, SEQ, D), 1.0),
        'c': nrm(ks[1], (BATCH, D), 1.0),
        'ctx': nrm(ks[2], (BATCH, CTX_LEN, D), 1.0),
        'c_ctx': nrm(ks[3], (D,), 1.0),
        'ada_w': nrm(ks[4], (DEPTH, D, 6 * D), 0.5 * D ** -0.5),
        'ada_b': nrm(ks[5], (DEPTH, 6 * D), 0.02),
        'norm1_g': 1.0 + nrm(ks[6], (DEPTH, D), 0.05),
        'norm2_g': 1.0 + nrm(ks[7], (DEPTH, D), 0.05),
        'w_in': nrm(ks[8], (DEPTH, D, D_IN), D ** -0.5),
        'attn_q_norm_g': 1.0 + nrm(ks[9], (DEPTH, HEAD_DIM), 0.05),
        'attn_k_norm_g': 1.0 + nrm(ks[10], (DEPTH, HEAD_DIM), 0.05),
        'attn_sink': nrm(ks[11], (DEPTH, ATT_HEADS), 0.5),
        'gla_gate_w': nrm(ks[12], (DEPTH, 2, GLA_RANK, GLA_HEADS * GLA_DK), GLA_RANK ** -0.5),
        'gla_gate_b': nrm(ks[13], (DEPTH, 2, GLA_HEADS * GLA_DK), 0.1),
        'gla_out_norm_g': 1.0 + nrm(ks[14], (DEPTH, GLA_DV), 0.05),
        'ret_log_decay': ret_base[None, None, :] + nrm(ks[15], (DEPTH, 2, RET_HEADS), 0.1),
        'ret_out_norm_g': 1.0 + nrm(ks[16], (DEPTH, RET_DV), 0.05),
        'w_branch': nrm(ks[17], (DEPTH, N_BRANCH, BRANCH_W, D), BRANCH_W ** -0.5),
        'w_out': nrm(ks[18], (DEPTH, D, D), D ** -0.5),
        'ffn_up': nrm(ks[19], (DEPTH, D, 2 * D_FF), D ** -0.5),
        'ffn_conv_w': nrm(ks[20], (DEPTH, CONV_W, 2 * D_FF), CONV_W ** -0.5),
        'ffn_conv_b': nrm(ks[21], (DEPTH, 2 * D_FF), 0.02),
        'ffn_down': nrm(ks[22], (DEPTH, D_FF, D), D_FF ** -0.5),
    }


def reference(x, c, ctx, c_ctx, ada_w, ada_b, norm1_g, norm2_g, w_in, attn_q_norm_g, attn_k_norm_g,
              attn_sink, gla_gate_w, gla_gate_b, gla_out_norm_g, ret_log_decay, ret_out_norm_g,
              w_branch, w_out, ffn_up, ffn_conv_w, ffn_conv_b, ffn_down):
    cos, sin = axial_rope_tables(x.shape[1])
    for l in range(DEPTH):
        with_ctx = l < DEPTH - 1
        mod = (jax.nn.silu(c) @ ada_w[l] + ada_b[l])[:, None, :]
        mod_c = jax.nn.silu(c_ctx) @ ada_w[l] + ada_b[l]
        sh1, sc1, gt1, sh2, sc2, gt2 = jnp.split(mod, 6, axis=-1)
        csh1, csc1, cgt1, csh2, csc2, cgt2 = jnp.split(mod_c, 6, axis=-1)

        h = rms_norm(x, norm1_g[l]) * (1.0 + sc1) + sh1
        hc = rms_norm(ctx, norm1_g[l]) * (1.0 + csc1) + csh1
        y, yc = token_mixers(h, hc, w_in[l], attn_q_norm_g[l], attn_k_norm_g[l], attn_sink[l],
                             gla_gate_w[l], gla_gate_b[l], gla_out_norm_g[l], ret_log_decay[l],
                             ret_out_norm_g[l], w_branch[l], w_out[l], cos, sin, with_ctx)
        x = x + gt1 * y
        h2 = rms_norm(x, norm2_g[l]) * (1.0 + sc2) + sh2
        x = x + gt2 * conv_ffn(h2, ffn_up[l], ffn_conv_w[l], ffn_conv_b[l], ffn_down[l])
        if with_ctx:
            ctx = ctx + cgt1 * yc
            hc2 = rms_norm(ctx, norm2_g[l]) * (1.0 + csc2) + csh2
            ctx = ctx + cgt2 * conv_ffn(hc2, ffn_up[l], ffn_conv_w[l], ffn_conv_b[l], ffn_down[l])
    return x
```

```python
import functools

import jax
import jax.numpy as jnp
import numpy as np
from jax import lax
from jax.experimental import pallas as pl
from jax.experimental.pallas import tpu as pltpu

F32 = jnp.float32
BF16 = jnp.bfloat16

D_MODEL = 1024
BATCH = 2
SEQ = 8192
DEPTH = 4
CTX_LEN = 256
GRID_W = 64
EPS = 1e-6

ATT_HEADS = 8
ATT_KV_HEADS = 2
ATT_GROUP = ATT_HEADS // ATT_KV_HEADS
HEAD_DIM = 64
WBLK = 128
ROPE_BASE = 10000.0
ROPE_AXIS_DIM = HEAD_DIM // 2
ROPE_FREQS = ROPE_AXIS_DIM // 2

GLA_HEADS = 4
GLA_DK = 64
GLA_DV = 128
GLA_RANK = 16
GLA_TAU = 16.0
RET_HEADS = 4
RET_DK = 64
RET_DV = 128
CHUNK = 64
N_BRANCH = 3
BRANCH_W = 512
D_FF = 2816

LANES = 128
SUBLANES = 8
VMEM_LIMIT = 56 * 1024 * 1024

ROWS_B = CTX_LEN + SEQ
NT = BATCH * ROWS_B
TM = 256
TILES_B = ROWS_B // TM
N_TILES = NT // TM
ABLK_B = ROWS_B // WBLK
CTX_ABLK = CTX_LEN // WBLK
SBLK = 256
CH_PER_BLK = SBLK // CHUNK
FT = 256
HALO = SUBLANES

P1_W = 512 + 128 + 128 + 256 + 256 + 512 + 256 + 256 + 512 + LANES
P2_W = 512 + 512 + N_BRANCH * D_MODEL


def _const_spec(shape):
    nd = len(shape)
    return pl.BlockSpec(shape, lambda *_: (0,) * nd, pipeline_mode=pl.Buffered(1))


def _params(sem):
    return pltpu.CompilerParams(dimension_semantics=sem, vmem_limit_bytes=VMEM_LIMIT)


def _sigmoid(x):
    return 1.0 / (1.0 + jnp.exp(-x))


def _silu(x):
    return x * _sigmoid(x)


def _mod_norm(x, g, sc, sh):
    ms = jnp.mean(x * x, axis=-1, keepdims=True)
    return (x * lax.rsqrt(ms + EPS) * g) * (1.0 + sc) + sh


def _mod_index(i):
    return 2 * (i // TILES_B) + jnp.minimum(i % TILES_B, 1)


def _ada_kernel(c_ref, w_ref, b_ref, o_ref):
    a = _silu(c_ref[...]).astype(BF16)
    o_ref[0] = jnp.dot(a, w_ref[0].astype(BF16), preferred_element_type=F32) + b_ref[0]


def _ada_mods(c_rows, ada_w, ada_b):
    n_col = 6 * D_MODEL // 1024
    return pl.pallas_call(
        _ada_kernel,
        grid=(DEPTH, n_col),
        in_specs=[
            pl.BlockSpec((SUBLANES, D_MODEL), lambda l, j: (0, 0)),
            pl.BlockSpec((1, D_MODEL, 1024), lambda l, j: (l, 0, j)),
            pl.BlockSpec((1, 1, 1024), lambda l, j: (l, 0, j)),
        ],
        out_specs=pl.BlockSpec((1, SUBLANES, 1024), lambda l, j: (l, 0, j)),
        out_shape=jax.ShapeDtypeStruct((DEPTH, SUBLANES, 6 * D_MODEL), F32),
        compiler_params=_params(("arbitrary", "arbitrary")),
        name="ada_mods",
    )(c_rows, ada_w, ada_b.reshape(DEPTH, 1, 6 * D_MODEL))


def _head_rms(x, mavg):
    x2 = x * x
    hi = x2.astype(BF16)
    lo = (x2 - hi.astype(F32)).astype(BF16)
    return jnp.dot(hi, mavg, preferred_element_type=F32) + jnp.dot(lo, mavg, preferred_element_type=F32)


def _rope(x, cos, sin, first_half):
    up = pltpu.roll(x, LANES - ROPE_FREQS, 1)
    dn = pltpu.roll(x, ROPE_FREQS, 1)
    return x * cos + jnp.where(first_half, up, dn) * sin


def _inproj_kernel(x_ref, mod_ref, n1_ref, w_ref, cos_ref, sin_ref, qg_ref, kg_ref, mavg_ref, wg_ref, gb_ref,
                   qa_ref, ka_ref, va_ref, gq_ref, gk_ref, gv_ref, gl_ref, rq_ref, rk_ref, rv_ref):
    sh = mod_ref[0, 0:1, :]
    sc = mod_ref[0, 1:2, :]
    hb = _mod_norm(x_ref[...], n1_ref[...], sc, sh).astype(BF16)
    cos = cos_ref[...]
    sin = sin_ref[...]
    mavg = mavg_ref[...]
    lane = lax.broadcasted_iota(jnp.int32, (TM, LANES), 1)
    first_half = (lane % ROPE_AXIS_DIM) < ROPE_FREQS

    def proj(lo, hi):
        return jnp.dot(hb, w_ref[:, lo:hi], preferred_element_type=F32)

    def norm_rope(x, g):
        xn = x * lax.rsqrt(_head_rms(x, mavg) + EPS) * g
        return _rope(xn, cos, sin, first_half)

    za = proj(0, 768)
    for cblk in range(4):
        q = norm_rope(za[:, cblk * LANES:(cblk + 1) * LANES], qg_ref[...])
        qa_ref[:, cblk * LANES:(cblk + 1) * LANES] = (q * HEAD_DIM ** -0.5).astype(BF16)
    ka_ref[...] = norm_rope(za[:, 512:640], kg_ref[...]).astype(BF16)
    va_ref[...] = za[:, 640:768].astype(BF16)

    zb = proj(768, 1792)
    gq_ref[...] = (zb[:, 0:256] * GLA_DK ** -0.5).astype(BF16)
    gk_ref[...] = zb[:, 256:512].astype(BF16)
    gv_ref[...] = zb[:, 512:1024].astype(BF16)

    zc = proj(1792, 2816)
    for cblk in range(2):
        sl = slice(cblk * LANES, (cblk + 1) * LANES)
        rq_ref[:, sl] = _rope(zc[:, sl], cos, sin, first_half).astype(BF16)
        rk = zc[:, 256 + cblk * LANES:256 + (cblk + 1) * LANES] * RET_DK ** -0.5
        rk_ref[:, sl] = _rope(rk, cos, sin, first_half).astype(BF16)
    rv_ref[...] = zc[:, 512:1024].astype(BF16)

    lr = proj(2816, 2944).astype(BF16)
    gl = jnp.dot(lr, wg_ref[...], preferred_element_type=F32) + gb_ref[...]
    gl_ref[...] = -(jnp.maximum(-gl, 0.0) + jnp.log1p(jnp.exp(-jnp.abs(gl)))) * (1.0 / GLA_TAU)


def _inproj(xc, mod, n1g, w1, cos_t, sin_t, qg, kg, mavg, wg, gbias):
    row = lambda w: pl.BlockSpec((TM, w), lambda i: (i, 0))
    widths = (512, 128, 128, 256, 256, 512, 512, 256, 256, 512)
    dtypes = (BF16, BF16, BF16, BF16, BF16, BF16, F32, BF16, BF16, BF16)
    return pl.pallas_call(
        _inproj_kernel,
        grid=(N_TILES,),
        in_specs=[
            row(D_MODEL),
            pl.BlockSpec((1, 6, D_MODEL), lambda i: (_mod_index(i), 0, 0)),
            _const_spec((1, D_MODEL)),
            _const_spec((D_MODEL, P1_W)),
            pl.BlockSpec((TM, LANES), lambda i: (i % TILES_B, 0)),
            pl.BlockSpec((TM, LANES), lambda i: (i % TILES_B, 0)),
            _const_spec((1, LANES)),
            _const_spec((1, LANES)),
            _const_spec((LANES, LANES)),
            _const_spec((LANES, 512)),
            _const_spec((1, 512)),
        ],
        out_specs=[row(w) for w in widths],
        out_shape=[jax.ShapeDtypeStruct((NT, w), d) for w, d in zip(widths, dtypes)],
        compiler_params=_params(("arbitrary",)),
        name="in_proj",
    )(xc, mod, n1g, w1, cos_t, sin_t, qg, kg, mavg, wg, gbias)


def _attn_kernel(sink_ref, q_ref, kl_ref, ks_ref, kr_ref, kc_ref, vl_ref, vs_ref, vr_ref, vc_ref, o_ref):
    r = pl.program_id(1)
    n_loc = 3 * WBLK
    ii = lax.broadcasted_iota(jnp.int32, (WBLK, n_loc + CTX_LEN), 0)
    cc = lax.broadcasted_iota(jnp.int32, (WBLK, n_loc + CTX_LEN), 1)
    rel = cc - WBLK - ii
    kblk = r - 1 + cc // WBLK
    local_ok = ((jnp.abs(rel) <= WBLK) & (kblk >= CTX_ABLK) & (kblk <= ABLK_B - 1) & (r >= CTX_ABLK))
    mask = (cc >= n_loc) | local_ok

    k_all = jnp.concatenate([kl_ref[...], ks_ref[...], kr_ref[...], kc_ref[...]], axis=0)
    v_all = jnp.concatenate([vl_ref[...], vs_ref[...], vr_ref[...], vc_ref[...]], axis=0)
    q = q_ref[...]
    outs = []
    for h in range(ATT_HEADS):
        kh = h // ATT_GROUP
        k_h = k_all[:, kh * HEAD_DIM:(kh + 1) * HEAD_DIM]
        v_h = v_all[:, kh * HEAD_DIM:(kh + 1) * HEAD_DIM]
        q_h = q[:, h * HEAD_DIM:(h + 1) * HEAD_DIM]
        s = lax.dot_general(q_h, k_h, (((1,), (1,)), ((), ())), preferred_element_type=F32)
        s = jnp.where(mask, s, -jnp.inf)
        sink = sink_ref[h]
        m = jnp.maximum(jnp.max(s, axis=-1, keepdims=True), sink)
        e = jnp.exp(s - m)
        denom = jnp.sum(e, axis=-1, keepdims=True) + jnp.exp(sink - m)
        o = jnp.dot(e.astype(BF16), v_h, preferred_element_type=F32)
        outs.append(o * (1.0 / denom))
    o_ref[...] = jnp.concatenate(outs, axis=1).astype(BF16)


def _attention(sink, qa, ka, va):
    blk = lambda f: pl.BlockSpec((WBLK, LANES), f)
    left = lambda b, r: (b * ABLK_B + jnp.maximum(r - 1, 0), 0)
    here = lambda b, r: (b * ABLK_B + r, 0)
    right = lambda b, r: (b * ABLK_B + jnp.minimum(r + 1, ABLK_B - 1), 0)
    ctx = pl.BlockSpec((CTX_LEN, LANES), lambda b, r: (b * TILES_B, 0))
    return pl.pallas_call(
        _attn_kernel,
        grid=(BATCH, ABLK_B),
        in_specs=[
            pl.BlockSpec(memory_space=pltpu.SMEM),
            pl.BlockSpec((WBLK, 512), here),
            blk(left), blk(here), blk(right), ctx,
            blk(left), blk(here), blk(right), ctx,
        ],
        out_specs=pl.BlockSpec((WBLK, 512), here),
        out_shape=jax.ShapeDtypeStruct((NT, 512), BF16),
        compiler_params=_params(("arbitrary", "arbitrary")),
        name="window_attn",
    )(sink, qa, ka, ka, ka, ka, va, va, va, va)


def _split3(g):
    hi = g.astype(BF16)
    r1 = g - hi.astype(F32)
    mid = r1.astype(BF16)
    lo = (r1 - mid.astype(F32)).astype(BF16)
    return hi, mid, lo


def _scan_direction(q_ref, k_ref, v_ref, b, tot_rows, s_ref, o_ref, reverse):
    n_h = GLA_HEADS
    ii = lax.broadcasted_iota(jnp.int32, (SBLK, SBLK), 0)
    jj = lax.broadcasted_iota(jnp.int32, (SBLK, SBLK), 1)
    same_chunk = (ii // CHUNK) == (jj // CHUNK)
    tri = jnp.logical_and(same_chunk, (jj >= ii) if reverse else (jj <= ii))

    tot = jnp.concatenate([jnp.broadcast_to(t, (CHUNK, n_h * GLA_DK)) for t in tot_rows], axis=0)
    q = q_ref[...].astype(F32)
    k = k_ref[...].astype(F32)
    qt = (q * jnp.exp(b)).astype(BF16)
    kt = (k * jnp.exp(-b)).astype(BF16)
    kend_t = (k * jnp.exp(tot - b)).T.astype(BF16)
    row_id = lax.broadcasted_iota(jnp.int32, (LANES, n_h * GLA_DK), 0)
    tot_mat = jnp.zeros((LANES, n_h * GLA_DK), F32)
    for c, t in enumerate(tot_rows):
        tot_mat = jnp.where(row_id == c, t, tot_mat)
    dec_t = jnp.exp(tot_mat.T)

    intra = []
    for h in range(n_h):
        ks = slice(h * GLA_DK, (h + 1) * GLA_DK)
        att = lax.dot_general(qt[:, ks], kt[:, ks], (((1,), (1,)), ((), ())), preferred_element_type=F32)
        att = jnp.where(tri, att, 0.0).astype(BF16)
        intra.append(jnp.dot(att, v_ref[:, h * GLA_DV:(h + 1) * GLA_DV], preferred_element_type=F32))

    order = range(CH_PER_BLK - 1, -1, -1) if reverse else range(CH_PER_BLK)
    for c in order:
        rows = slice(c * CHUNK, (c + 1) * CHUNK)
        for h in range(n_h):
            ks = slice(h * GLA_DK, (h + 1) * GLA_DK)
            vs = slice(h * GLA_DV, (h + 1) * GLA_DV)
            s_h = s_ref[ks, :]
            inter = jnp.dot(qt[rows, ks], s_h.astype(BF16), preferred_element_type=F32)
            o_ref[rows, vs] = intra[h][rows, :] + inter
            ds = jnp.dot(kend_t[ks, rows], v_ref[rows, vs], preferred_element_type=F32)
            s_ref[ks, :] = s_h * dec_t[ks, c:c + 1] + ds


def _gla_kernel(qf_ref, kf_ref, vf_ref, gf_ref, qb_ref, kb_ref, vb_ref, gb_ref, lf_ref, lb_ref,
                of_ref, ob_ref, sf_ref, sb_ref):
    @pl.when(pl.program_id(1) == 0)
    def _():
        sf_ref[...] = jnp.zeros_like(sf_ref)
        sb_ref[...] = jnp.zeros_like(sb_ref)

    def cum(g, tri_ref):
        tri = tri_ref[...]
        return sum(jnp.dot(tri, part, preferred_element_type=F32) for part in _split3(g))

    bf = cum(gf_ref[...], lf_ref)
    tot_f = [bf[(c + 1) * CHUNK - 1:(c + 1) * CHUNK, :] for c in range(CH_PER_BLK)]
    _scan_direction(qf_ref, kf_ref, vf_ref, bf, tot_f, sf_ref, of_ref, reverse=False)
    bb = cum(gb_ref[...], lb_ref)
    tot_b = [bb[c * CHUNK:c * CHUNK + 1, :] for c in range(CH_PER_BLK)]
    _scan_direction(qb_ref, kb_ref, vb_ref, bb, tot_b, sb_ref, ob_ref, reverse=True)


def _ret_kernel(qf_ref, kf_ref, vf_ref, qb_ref, kb_ref, vb_ref, ld_ref, of_ref, ob_ref, sf_ref, sb_ref):
    @pl.when(pl.program_id(1) == 0)
    def _():
        sf_ref[...] = jnp.zeros_like(sf_ref)
        sb_ref[...] = jnp.zeros_like(sb_ref)

    pos = (lax.broadcasted_iota(jnp.int32, (SBLK, RET_HEADS * RET_DK), 0) % CHUNK).astype(F32)
    lg_f = -jnp.exp(ld_ref[0:1, :])
    lg_b = -jnp.exp(ld_ref[1:2, :])
    _scan_direction(qf_ref, kf_ref, vf_ref, (pos + 1.0) * lg_f, [lg_f * CHUNK] * CH_PER_BLK,
                    sf_ref, of_ref, reverse=False)
    _scan_direction(qb_ref, kb_ref, vb_ref, (CHUNK - pos) * lg_b, [lg_b * CHUNK] * CH_PER_BLK,
                    sb_ref, ob_ref, reverse=True)


def _bwd_block(s):
    return jnp.where(s == 0, 0, TILES_B - s)


def _scan_specs(width):
    fwd = pl.BlockSpec((SBLK, width), lambda b, s: (b * TILES_B + s, 0))
    bwd = pl.BlockSpec((SBLK, width), lambda b, s: (b * TILES_B + _bwd_block(s), 0))
    return fwd, bwd


def _scan_call(kernel, name, inputs, in_specs):
    of, ob = _scan_specs(512)
    return pl.pallas_call(
        kernel,
        grid=(BATCH, TILES_B),
        in_specs=in_specs,
        out_specs=[of, ob],
        out_shape=[jax.ShapeDtypeStruct((NT, 512), F32)] * 2,
        scratch_shapes=[pltpu.VMEM((GLA_HEADS * GLA_DK, GLA_DV), F32)] * 2,
        compiler_params=_params(("arbitrary", "arbitrary")),
        name=name,
    )(*inputs)


def _gla_scan(gq, gk, gv, glog, tri_f, tri_b):
    qf, qb = _scan_specs(256)
    vf, vb = _scan_specs(512)
    gf = pl.BlockSpec((SBLK, 256), lambda b, s: (b * TILES_B + s, 0))
    gb = pl.BlockSpec((SBLK, 256), lambda b, s: (b * TILES_B + _bwd_block(s), 1))
    tri = _const_spec((SBLK, SBLK))
    return _scan_call(_gla_kernel, "gla_scan", (gq, gk, gv, glog, gq, gk, gv, glog, tri_f, tri_b),
                      [qf, qf, vf, gf, qb, qb, vb, gb, tri, tri])


def _ret_scan(rq, rk, rv, ld):
    qf, qb = _scan_specs(256)
    vf, vb = _scan_specs(512)
    return _scan_call(_ret_kernel, "ret_scan", (rq, rk, rv, rq, rk, rv, ld),
                      [qf, qf, vf, qb, qb, vb, _const_spec((2, 256))])


def _merge_kernel(x_ref, mod_ref, n1_ref, w2_ref, oa_ref, gf_ref, gb_ref, rf_ref, rb_ref, gn_ref, rn_ref,
                  wbr_ref, wo_ref, y_ref):
    x = x_ref[...]
    sh = mod_ref[0, 0:1, :]
    sc = mod_ref[0, 1:2, :]
    gt = mod_ref[0, 2:3, :]
    hb = _mod_norm(x, n1_ref[...], sc, sh).astype(BF16)

    def gated(o, gate, ng):
        parts = []
        for h in range(4):
            sl = slice(h * LANES, (h + 1) * LANES)
            oh = o[:, sl]
            ms = jnp.mean(oh * oh, axis=-1, keepdims=True)
            parts.append((oh * lax.rsqrt(ms + EPS) * ng) * _silu(gate[:, sl]))
        return jnp.concatenate(parts, axis=1).astype(BF16)

    gates = jnp.dot(hb, w2_ref[:, 0:1024], preferred_element_type=F32)
    branches = (
        oa_ref[...],
        gated(gf_ref[...] + gb_ref[...], gates[:, 0:512], gn_ref[...]),
        gated(rf_ref[...] + rb_ref[...], gates[:, 512:1024], rn_ref[...]),
    )
    acc = jnp.zeros((TM, D_MODEL), F32)
    for z in range(N_BRANCH):
        mg = jnp.dot(hb, w2_ref[:, 1024 + z * D_MODEL:1024 + (z + 1) * D_MODEL], preferred_element_type=F32)
        acc = acc + _sigmoid(mg) * jnp.dot(branches[z], wbr_ref[z], preferred_element_type=F32)
    y = jnp.dot(acc.astype(BF16), wo_ref[...], preferred_element_type=F32)
    y_ref[...] = x + gt * y


def _merge(xc, mod, n1g, w2, oa, gf, gb, rf, rb, gng, rng, wbr, wo):
    row = lambda w: pl.BlockSpec((TM, w), lambda i: (i, 0))
    return pl.pallas_call(
        _merge_kernel,
        grid=(N_TILES,),
        in_specs=[
            row(D_MODEL),
            pl.BlockSpec((1, 6, D_MODEL), lambda i: (_mod_index(i), 0, 0)),
            _const_spec((1, D_MODEL)),
            _const_spec((D_MODEL, P2_W)),
            row(512), row(512), row(512), row(512), row(512),
            _const_spec((1, LANES)),
            _const_spec((1, LANES)),
            _const_spec((N_BRANCH, BRANCH_W, D_MODEL)),
            _const_spec((D_MODEL, D_MODEL)),
        ],
        out_specs=row(D_MODEL),
        out_shape=jax.ShapeDtypeStruct((NT, D_MODEL), F32),
        compiler_params=_params(("arbitrary",)),
        name="merge_out",
    )(xc, mod, n1g, w2, oa, gf, gb, rf, rb, gng, rng, wbr, wo)


def _ffn_kernel(x_ref, xp_ref, xn_ref, mod_ref, n2_ref, wup_ref, cw_ref, cb_ref, wdn_ref, y_ref):
    j = pl.program_id(0) % TILES_B
    sh = mod_ref[0, 3:4, :]
    sc = mod_ref[0, 4:5, :]
    gt = mod_ref[0, 5:6, :]
    g2 = n2_ref[...]
    x = x_ref[...]
    prev_ok = j >= 2
    next_ok = jnp.logical_and(j >= 1, j <= TILES_B - 2)
    hp = jnp.where(prev_ok, _mod_norm(xp_ref[...], g2, sc, sh), 0.0)
    hn = jnp.where(next_ok, _mod_norm(xn_ref[...], g2, sc, sh), 0.0)
    h = jnp.concatenate([hp, _mod_norm(x, g2, sc, sh), hn], axis=0).astype(BF16)

    def conv(col):
        u = jnp.dot(h, wup_ref[:, col:col + FT], preferred_element_type=F32)
        w = cw_ref[:, col:col + FT]
        prev = pltpu.roll(u, 1, 0)[HALO:HALO + TM]
        nxt = pltpu.roll(u, TM + 2 * HALO - 1, 0)[HALO:HALO + TM]
        return prev * w[0:1] + u[HALO:HALO + TM] * w[1:2] + nxt * w[2:3] + cb_ref[:, col:col + FT]

    acc = jnp.zeros((TM, D_MODEL), F32)
    for t in range(D_FF // FT):
        a = conv(t * FT)
        bv = conv(D_FF + t * FT)
        g = (_silu(a) * bv).astype(BF16)
        acc = acc + jnp.dot(g, wdn_ref[t * FT:(t + 1) * FT, :], preferred_element_type=F32)
    y_ref[...] = x + gt * acc


def _ffn(xc, mod, n2g, wup, cw, cb, wdn):
    per_tile = TM // HALO
    return pl.pallas_call(
        _ffn_kernel,
        grid=(N_TILES,),
        in_specs=[
            pl.BlockSpec((TM, D_MODEL), lambda i: (i, 0)),
            pl.BlockSpec((HALO, D_MODEL), lambda i: (jnp.maximum(i * per_tile - 1, 0), 0)),
            pl.BlockSpec((HALO, D_MODEL), lambda i: (jnp.minimum((i + 1) * per_tile, NT // HALO - 1), 0)),
            pl.BlockSpec((1, 6, D_MODEL), lambda i: (_mod_index(i), 0, 0)),
            _const_spec((1, D_MODEL)),
            _const_spec((D_MODEL, 2 * D_FF)),
            _const_spec((3, 2 * D_FF)),
            _const_spec((1, 2 * D_FF)),
            _const_spec((D_FF, D_MODEL)),
        ],
        out_specs=pl.BlockSpec((TM, D_MODEL), lambda i: (i, 0)),
        out_shape=jax.ShapeDtypeStruct((NT, D_MODEL), F32),
        compiler_params=_params(("arbitrary",)),
        name="conv_ffn",
    )(xc, xc, xc, mod, n2g, wup, cw, cb, wdn)


def _rope_tables():
    t = jnp.arange(SEQ)
    row = (t // GRID_W).astype(F32)
    col = (t % GRID_W).astype(F32)
    inv = ROPE_BASE ** (-jnp.arange(ROPE_FREQS, dtype=F32) * 2.0 / ROPE_AXIS_DIM)
    ang_r = row[:, None] * inv
    ang_c = col[:, None] * inv
    cos_r, sin_r, cos_c, sin_c = jnp.cos(ang_r), jnp.sin(ang_r), jnp.cos(ang_c), jnp.sin(ang_c)
    cos = jnp.concatenate([cos_r, cos_r, cos_c, cos_c], axis=1)
    sin = jnp.concatenate([-sin_r, sin_r, -sin_c, sin_c], axis=1)
    cos = jnp.concatenate([jnp.ones((CTX_LEN, HEAD_DIM), F32), cos], axis=0)
    sin = jnp.concatenate([jnp.zeros((CTX_LEN, HEAD_DIM), F32), sin], axis=0)
    return jnp.tile(cos, (1, 2)), jnp.tile(sin, (1, 2))


def _split_w_in(w_in):
    sizes = (512, 128, 128, 256, 256, 512, 512, 2 * GLA_RANK, 256, 256, 512, 512, N_BRANCH * D_MODEL)
    idx = [int(i) for i in np.cumsum(sizes)[:-1]]
    aq, ak, av, gq, gk, gv, gr, ga, rq, rk, rv, rg, mg = jnp.split(w_in, idx, axis=-1)
    pad = jnp.zeros(w_in.shape[:-1] + (LANES - 2 * GLA_RANK,), w_in.dtype)
    w1 = jnp.concatenate([aq, ak, av, gq, gk, gv, rq, rk, rv, ga, pad], axis=-1).astype(BF16)
    w2 = jnp.concatenate([gr, rg, mg], axis=-1).astype(BF16)
    return w1, w2


def kernel(x, c, ctx, c_ctx, ada_w, ada_b, norm1_g, norm2_g, w_in, attn_q_norm_g, attn_k_norm_g, attn_sink,
           gla_gate_w, gla_gate_b, gla_out_norm_g, ret_log_decay, ret_out_norm_g, w_branch, w_out, ffn_up,
           ffn_conv_w, ffn_conv_b, ffn_down):
    c_rows = jnp.concatenate([c, c_ctx[None], jnp.zeros((SUBLANES - BATCH - 1, D_MODEL), F32)], axis=0)
    mods = _ada_mods(c_rows, ada_w, ada_b).reshape(DEPTH, SUBLANES, 6, D_MODEL)
    pick = np.array([r for b in range(BATCH) for r in (BATCH, b)])
    mods = mods[:, pick]

    w1, w2 = _split_w_in(w_in)
    wbr = w_branch.astype(BF16)
    wo = w_out.astype(BF16)
    wup = ffn_up.astype(BF16)
    wdn = ffn_down.astype(BF16)
    cos_t, sin_t = _rope_tables()
    qg = jnp.tile(attn_q_norm_g, (1, 2)).reshape(DEPTH, 1, LANES)
    kg = jnp.tile(attn_k_norm_g, (1, 2)).reshape(DEPTH, 1, LANES)
    lane_group = np.arange(LANES) // HEAD_DIM
    mavg = jnp.asarray((lane_group[:, None] == lane_group[None, :]) / HEAD_DIM, BF16)
    wg = jnp.zeros((DEPTH, LANES, 512), F32)
    wg = wg.at[:, 0:GLA_RANK, 0:256].set(gla_gate_w[:, 0])
    wg = wg.at[:, GLA_RANK:2 * GLA_RANK, 256:512].set(gla_gate_w[:, 1]).astype(BF16)
    gbias = gla_gate_b.reshape(DEPTH, 1, 512)
    ld = jnp.repeat(ret_log_decay, RET_DK, axis=-1)
    rr = np.arange(SBLK)
    same = (rr[:, None] // CHUNK) == (rr[None, :] // CHUNK)
    tri_f = jnp.asarray(same & (rr[None, :] <= rr[:, None]), BF16)
    tri_b = jnp.asarray(same & (rr[None, :] >= rr[:, None]), BF16)

    xc = jnp.concatenate([ctx, x], axis=1).reshape(NT, D_MODEL)
    for l in range(DEPTH):
        n1g = norm1_g[l].reshape(1, D_MODEL)
        n2g = norm2_g[l].reshape(1, D_MODEL)
        qa, ka, va, gq, gk, gv, glog, rq, rk, rv = _inproj(
            xc, mods[l], n1g, w1[l], cos_t, sin_t, qg[l], kg[l], mavg, wg[l], gbias[l])
        oa = _attention(attn_sink[l], qa, ka, va)
        gf, gb = _gla_scan(gq, gk, gv, glog, tri_f, tri_b)
        rf, rb = _ret_scan(rq, rk, rv, ld[l])
        xc = _merge(xc, mods[l], n1g, w2[l], oa, gf, gb, rf, rb,
                    gla_out_norm_g[l].reshape(1, LANES), ret_out_norm_g[l].reshape(1, LANES), wbr[l], wo[l])
        xc = _ffn(xc, mods[l], n2g, wup[l], ffn_conv_w[l], ffn_conv_b[l].reshape(1, 2 * D_FF), wdn[l])
    return xc.reshape(BATCH, ROWS_B, D_MODEL)[:, CTX_LEN:]
```

```python
import functools

import jax
import jax.numpy as jnp
import numpy as np
from jax import lax
from jax.experimental import pallas as pl
from jax.experimental.pallas import tpu as pltpu

F32 = jnp.float32
BF16 = jnp.bfloat16

D_MODEL = 1024
BATCH = 2
SEQ = 8192
DEPTH = 4
CTX_LEN = 256
GRID_W = 64
EPS = 1e-6

ATT_HEADS = 8
ATT_KV_HEADS = 2
ATT_GROUP = ATT_HEADS // ATT_KV_HEADS
HEAD_DIM = 64
WBLK = 128
ROPE_BASE = 10000.0
ROPE_AXIS_DIM = HEAD_DIM // 2
ROPE_FREQS = ROPE_AXIS_DIM // 2

GLA_HEADS = 4
GLA_DK = 64
GLA_DV = 128
GLA_RANK = 16
GLA_TAU = 16.0
RET_HEADS = 4
RET_DK = 64
RET_DV = 128
CHUNK = 64
N_BRANCH = 3
BRANCH_W = 512
D_FF = 2816

LANES = 128
SUBLANES = 8
VMEM_LIMIT = 56 * 1024 * 1024

ROWS_B = CTX_LEN + SEQ
NT = BATCH * ROWS_B
TM = 256
TILES_B = ROWS_B // TM
N_TILES = NT // TM
ABLK_B = ROWS_B // WBLK
CTX_ABLK = CTX_LEN // WBLK
SBLK = 256
CH_PER_BLK = SBLK // CHUNK
FT = 256
HALO = SUBLANES

P1_W = 512 + 128 + 128 + 256 + 256 + 512 + 256 + 256 + 512 + LANES
P2_W = 512 + 512 + N_BRANCH * D_MODEL


def _const_spec(shape):
    nd = len(shape)
    return pl.BlockSpec(shape, lambda *_: (0,) * nd, pipeline_mode=pl.Buffered(1))


def _layer_spec(shape, l):
    nd = len(shape)
    return pl.BlockSpec((None,) + tuple(shape), lambda *_: (l,) + (0,) * nd, pipeline_mode=pl.Buffered(1))


def _params(sem):
    return pltpu.CompilerParams(dimension_semantics=sem, vmem_limit_bytes=VMEM_LIMIT)


def _sigmoid(x):
    return 1.0 / (1.0 + jnp.exp(-x))


def _silu(x):
    return x * _sigmoid(x)


def _mod_norm(x, g, sc, sh):
    ms = jnp.mean(x * x, axis=-1, keepdims=True)
    return (x * lax.rsqrt(ms + EPS) * g) * (1.0 + sc) + sh


def _mod_index(i):
    return 2 * (i // TILES_B) + jnp.minimum(i % TILES_B, 1)


def _ada_kernel(c_ref, w_ref, b_ref, o_ref):
    a = _silu(c_ref[...]).astype(BF16)
    o_ref[0] = jnp.dot(a, w_ref[0].astype(BF16), preferred_element_type=F32) + b_ref[0]


def _ada_mods(c_rows, ada_w, ada_b):
    n_col = 6 * D_MODEL // 1024
    return pl.pallas_call(
        _ada_kernel,
        grid=(DEPTH, n_col),
        in_specs=[
            pl.BlockSpec((SUBLANES, D_MODEL), lambda l, j: (0, 0)),
            pl.BlockSpec((1, D_MODEL, 1024), lambda l, j: (l, 0, j)),
            pl.BlockSpec((1, 1, 1024), lambda l, j: (l, 0, j)),
        ],
        out_specs=pl.BlockSpec((1, SUBLANES, 1024), lambda l, j: (l, 0, j)),
        out_shape=jax.ShapeDtypeStruct((DEPTH, SUBLANES, 6 * D_MODEL), F32),
        compiler_params=_params(("arbitrary", "arbitrary")),
        name="ada_mods",
    )(c_rows, ada_w, ada_b.reshape(DEPTH, 1, 6 * D_MODEL))


def _head_rms(x, mavg):
    x2 = x * x
    hi = x2.astype(BF16)
    lo = (x2 - hi.astype(F32)).astype(BF16)
    return jnp.dot(hi, mavg, preferred_element_type=F32) + jnp.dot(lo, mavg, preferred_element_type=F32)


def _rope(x, cos, sin, first_half):
    up = pltpu.roll(x, LANES - ROPE_FREQS, 1)
    dn = pltpu.roll(x, ROPE_FREQS, 1)
    return x * cos + jnp.where(first_half, up, dn) * sin


def _inproj_kernel(x_ref, mod_ref, n1_ref, w_ref, cos_ref, sin_ref, qg_ref, kg_ref, mavg_ref, wg_ref, gb_ref,
                   qa_ref, ka_ref, va_ref, gq_ref, gk_ref, gv_ref, gl_ref, rq_ref, rk_ref, rv_ref):
    sh = mod_ref[0, 0:1, :]
    sc = mod_ref[0, 1:2, :]
    hb = _mod_norm(x_ref[...], n1_ref[...], sc, sh).astype(BF16)
    cos = cos_ref[...]
    sin = sin_ref[...]
    mavg = mavg_ref[...]
    lane = lax.broadcasted_iota(jnp.int32, (TM, LANES), 1)
    first_half = (lane % ROPE_AXIS_DIM) < ROPE_FREQS

    def proj(lo, hi):
        return jnp.dot(hb, w_ref[:, lo:hi], preferred_element_type=F32)

    ze = proj(2816, 2944)
    za = proj(0, 768)
    gl = jnp.dot(ze.astype(BF16), wg_ref[...], preferred_element_type=F32) + gb_ref[...]
    zc = proj(1792, 2816)
    gl_ref[...] = -(jnp.maximum(-gl, 0.0) + jnp.log1p(jnp.exp(-jnp.abs(gl)))) * (1.0 / GLA_TAU)
    qk = [za[:, cblk * LANES:(cblk + 1) * LANES] for cblk in range(5)]
    ms = [_head_rms(t, mavg) for t in qk]
    zb = proj(768, 1792)

    for cblk in range(5):
        g = qg_ref[...] if cblk < 4 else kg_ref[...]
        t = _rope(qk[cblk] * lax.rsqrt(ms[cblk] + EPS) * g, cos, sin, first_half)
        if cblk < 4:
            qa_ref[:, cblk * LANES:(cblk + 1) * LANES] = (t * HEAD_DIM ** -0.5).astype(BF16)
        else:
            ka_ref[...] = t.astype(BF16)
    va_ref[...] = za[:, 640:768].astype(BF16)

    for cblk in range(2):
        sl = slice(cblk * LANES, (cblk + 1) * LANES)
        rq_ref[:, sl] = _rope(zc[:, sl], cos, sin, first_half).astype(BF16)
        rk = zc[:, 256 + cblk * LANES:256 + (cblk + 1) * LANES] * RET_DK ** -0.5
        rk_ref[:, sl] = _rope(rk, cos, sin, first_half).astype(BF16)
    rv_ref[...] = zc[:, 512:1024].astype(BF16)

    gq_ref[...] = (zb[:, 0:256] * GLA_DK ** -0.5).astype(BF16)
    gk_ref[...] = zb[:, 256:512].astype(BF16)
    gv_ref[...] = zb[:, 512:1024].astype(BF16)


def _mod_spec(l):
    return pl.BlockSpec((None, 1, 6, D_MODEL), lambda i: (l, _mod_index(i), 0, 0))


def _inproj(l, xc, mod, n1g, w1, cos_t, sin_t, qg, kg, mavg, wg, gbias):
    row = lambda w: pl.BlockSpec((TM, w), lambda i: (i, 0))
    widths = (512, 128, 128, 256, 256, 512, 512, 256, 256, 512)
    dtypes = (BF16, BF16, BF16, BF16, BF16, BF16, F32, BF16, BF16, BF16)
    return pl.pallas_call(
        _inproj_kernel,
        grid=(N_TILES,),
        in_specs=[
            row(D_MODEL),
            _mod_spec(l),
            _layer_spec((1, D_MODEL), l),
            _layer_spec((D_MODEL, P1_W), l),
            pl.BlockSpec((TM, LANES), lambda i: (i % TILES_B, 0)),
            pl.BlockSpec((TM, LANES), lambda i: (i % TILES_B, 0)),
            _layer_spec((1, LANES), l),
            _layer_spec((1, LANES), l),
            _const_spec((LANES, LANES)),
            _layer_spec((LANES, 512), l),
            _layer_spec((1, 512), l),
        ],
        out_specs=[row(w) for w in widths],
        out_shape=[jax.ShapeDtypeStruct((NT, w), d) for w, d in zip(widths, dtypes)],
        compiler_params=_params(("arbitrary",)),
        name="in_proj",
    )(xc, mod, n1g, w1, cos_t, sin_t, qg, kg, mavg, wg, gbias)


def _attn_kernel(l, sink_ref, band_ref, q_ref, kl_ref, ks_ref, kr_ref, kc_ref, vl_ref, vs_ref, vr_ref, vc_ref,
                 o_ref):
    r = pl.program_id(1)
    n_loc = 3 * WBLK
    cc = lax.broadcasted_iota(jnp.int32, (1, n_loc + CTX_LEN), 1)
    kblk = r - 1 + cc // WBLK
    blk_ok = (cc >= n_loc) | ((kblk >= CTX_ABLK) & (kblk <= ABLK_B - 1) & (r >= CTX_ABLK))
    bias = band_ref[...] + jnp.where(blk_ok, 0.0, -jnp.inf)

    k_all = jnp.concatenate([kl_ref[...], ks_ref[...], kr_ref[...], kc_ref[...]], axis=0)
    v_all = jnp.concatenate([vl_ref[...], vs_ref[...], vr_ref[...], vc_ref[...]], axis=0)
    q = q_ref[...]
    scores = []
    for kh in range(ATT_KV_HEADS):
        heads = range(kh * ATT_GROUP, (kh + 1) * ATT_GROUP)
        q_s = jnp.concatenate([q[:, h * HEAD_DIM:(h + 1) * HEAD_DIM] for h in heads], axis=0)
        k_h = k_all[:, kh * HEAD_DIM:(kh + 1) * HEAD_DIM]
        scores.append(lax.dot_general(q_s, k_h, (((1,), (1,)), ((), ())), preferred_element_type=F32))
    outs = []
    for kh in range(ATT_KV_HEADS):
        heads = range(kh * ATT_GROUP, (kh + 1) * ATT_GROUP)
        v_h = v_all[:, kh * HEAD_DIM:(kh + 1) * HEAD_DIM]
        sink = jnp.concatenate([jnp.full((1, WBLK, 1), sink_ref[l, h], F32) for h in heads], axis=0)
        s = scores[kh].reshape(ATT_GROUP, WBLK, n_loc + CTX_LEN) + bias[None]
        m = jnp.maximum(jnp.max(s, axis=-1, keepdims=True), sink)
        e = jnp.exp(s - m)
        denom = jnp.sum(e, axis=-1, keepdims=True) + jnp.exp(sink - m)
        e2 = e.astype(BF16).reshape(ATT_GROUP * WBLK, n_loc + CTX_LEN)
        o = jnp.dot(e2, v_h, preferred_element_type=F32).reshape(ATT_GROUP, WBLK, HEAD_DIM) * (1.0 / denom)
        outs.extend(o[g] for g in range(ATT_GROUP))
    o_ref[...] = jnp.concatenate(outs, axis=1).astype(BF16)


def _band_bias():
    ii = np.arange(WBLK)[:, None]
    cc = np.arange(3 * WBLK + CTX_LEN)[None, :]
    ok = (cc >= 3 * WBLK) | (np.abs(cc - WBLK - ii) <= WBLK)
    return jnp.asarray(np.where(ok, 0.0, -np.inf), F32)


def _attention(l, sink, qa, ka, va):
    blk = lambda f: pl.BlockSpec((WBLK, LANES), f)
    left = lambda b, r: (b * ABLK_B + jnp.maximum(r - 1, 0), 0)
    here = lambda b, r: (b * ABLK_B + r, 0)
    right = lambda b, r: (b * ABLK_B + jnp.minimum(r + 1, ABLK_B - 1), 0)
    ctx = pl.BlockSpec((CTX_LEN, LANES), lambda b, r: (b * TILES_B, 0))
    return pl.pallas_call(
        functools.partial(_attn_kernel, l),
        grid=(BATCH, ABLK_B),
        in_specs=[
            pl.BlockSpec(memory_space=pltpu.SMEM),
            _const_spec((WBLK, 3 * WBLK + CTX_LEN)),
            pl.BlockSpec((WBLK, 512), here),
            blk(left), blk(here), blk(right), ctx,
            blk(left), blk(here), blk(right), ctx,
        ],
        out_specs=pl.BlockSpec((WBLK, 512), here),
        out_shape=jax.ShapeDtypeStruct((NT, 512), BF16),
        compiler_params=_params(("arbitrary", "arbitrary")),
        name="window_attn",
    )(sink, _band_bias(), qa, ka, ka, ka, ka, va, va, va, va)


def _split3(g):
    hi = g.astype(BF16)
    r1 = g - hi.astype(F32)
    mid = r1.astype(BF16)
    lo = (r1 - mid.astype(F32)).astype(BF16)
    return hi, mid, lo


def _scan_direction(q_ref, k_ref, v_ref, b, tot_rows, s_ref, o_ref, reverse):
    n_h = GLA_HEADS
    ii = lax.broadcasted_iota(jnp.int32, (SBLK, SBLK), 0)
    jj = lax.broadcasted_iota(jnp.int32, (SBLK, SBLK), 1)
    same_chunk = (ii // CHUNK) == (jj // CHUNK)
    tri = jnp.logical_and(same_chunk, (jj >= ii) if reverse else (jj <= ii))

    tot = jnp.concatenate([jnp.broadcast_to(t, (CHUNK, n_h * GLA_DK)) for t in tot_rows], axis=0)
    q = q_ref[...].astype(F32)
    k = k_ref[...].astype(F32)
    qt = (q * jnp.exp(b)).astype(BF16)
    kt = (k * jnp.exp(-b)).astype(BF16)
    kend_t = (k * jnp.exp(tot - b)).T.astype(BF16)
    row_id = lax.broadcasted_iota(jnp.int32, (LANES, n_h * GLA_DK), 0)
    tot_mat = jnp.zeros((LANES, n_h * GLA_DK), F32)
    for c, t in enumerate(tot_rows):
        tot_mat = jnp.where(row_id == c, t, tot_mat)
    dec_t = jnp.exp(tot_mat.T)

    k_sl = [slice(h * GLA_DK, (h + 1) * GLA_DK) for h in range(n_h)]
    v_sl = [slice(h * GLA_DV, (h + 1) * GLA_DV) for h in range(n_h)]
    r_sl = [slice(c * CHUNK, (c + 1) * CHUNK) for c in range(CH_PER_BLK)]

    att = [lax.dot_general(qt[:, ks], kt[:, ks], (((1,), (1,)), ((), ())), preferred_element_type=F32)
           for ks in k_sl]
    ds = [[jnp.dot(kend_t[ks, rows], v_ref[rows, vs], preferred_element_type=F32)
           for ks, vs in zip(k_sl, v_sl)] for rows in r_sl]
    intra = [jnp.dot(jnp.where(tri, a, 0.0).astype(BF16), v_ref[:, vs], preferred_element_type=F32)
             for a, vs in zip(att, v_sl)]

    order = range(CH_PER_BLK - 1, -1, -1) if reverse else range(CH_PER_BLK)
    state = [s_ref[ks, :] for ks in k_sl]
    start = [None] * CH_PER_BLK
    for c in order:
        start[c] = [s.astype(BF16) for s in state]
        state = [s * dec_t[ks, c:c + 1] + d for s, ks, d in zip(state, k_sl, ds[c])]
    for ks, s in zip(k_sl, state):
        s_ref[ks, :] = s

    for c, rows in enumerate(r_sl):
        for h in range(n_h):
            inter = jnp.dot(qt[rows, k_sl[h]], start[c][h], preferred_element_type=F32)
            o_ref[rows, v_sl[h]] = intra[h][rows, :] + inter


def _gla_kernel(qf_ref, kf_ref, vf_ref, gf_ref, qb_ref, kb_ref, vb_ref, gb_ref, lf_ref, lb_ref,
                of_ref, ob_ref, sf_ref, sb_ref):
    @pl.when(pl.program_id(1) == 0)
    def _():
        sf_ref[...] = jnp.zeros_like(sf_ref)
        sb_ref[...] = jnp.zeros_like(sb_ref)

    def cum(g, tri_ref):
        tri = tri_ref[...]
        return sum(jnp.dot(tri, part, preferred_element_type=F32) for part in _split3(g))

    bf = cum(gf_ref[...], lf_ref)
    tot_f = [bf[(c + 1) * CHUNK - 1:(c + 1) * CHUNK, :] for c in range(CH_PER_BLK)]
    _scan_direction(qf_ref, kf_ref, vf_ref, bf, tot_f, sf_ref, of_ref, reverse=False)
    bb = cum(gb_ref[...], lb_ref)
    tot_b = [bb[c * CHUNK:c * CHUNK + 1, :] for c in range(CH_PER_BLK)]
    _scan_direction(qb_ref, kb_ref, vb_ref, bb, tot_b, sb_ref, ob_ref, reverse=True)


def _ret_kernel(qf_ref, kf_ref, vf_ref, qb_ref, kb_ref, vb_ref, ld_ref, of_ref, ob_ref, sf_ref, sb_ref):
    @pl.when(pl.program_id(1) == 0)
    def _():
        sf_ref[...] = jnp.zeros_like(sf_ref)
        sb_ref[...] = jnp.zeros_like(sb_ref)

    pos = (lax.broadcasted_iota(jnp.int32, (SBLK, RET_HEADS * RET_DK), 0) % CHUNK).astype(F32)
    lg_f = -jnp.exp(ld_ref[0:1, :])
    lg_b = -jnp.exp(ld_ref[1:2, :])
    _scan_direction(qf_ref, kf_ref, vf_ref, (pos + 1.0) * lg_f, [lg_f * CHUNK] * CH_PER_BLK,
                    sf_ref, of_ref, reverse=False)
    _scan_direction(qb_ref, kb_ref, vb_ref, (CHUNK - pos) * lg_b, [lg_b * CHUNK] * CH_PER_BLK,
                    sb_ref, ob_ref, reverse=True)


def _bwd_block(s):
    return jnp.where(s == 0, 0, TILES_B - s)


def _scan_specs(width):
    fwd = pl.BlockSpec((SBLK, width), lambda b, s: (b * TILES_B + s, 0))
    bwd = pl.BlockSpec((SBLK, width), lambda b, s: (b * TILES_B + _bwd_block(s), 0))
    return fwd, bwd


def _scan_call(kernel, name, inputs, in_specs):
    of, ob = _scan_specs(512)
    return pl.pallas_call(
        kernel,
        grid=(BATCH, TILES_B),
        in_specs=in_specs,
        out_specs=[of, ob],
        out_shape=[jax.ShapeDtypeStruct((NT, 512), F32)] * 2,
        scratch_shapes=[pltpu.VMEM((GLA_HEADS * GLA_DK, GLA_DV), F32)] * 2,
        compiler_params=_params(("arbitrary", "arbitrary")),
        name=name,
    )(*inputs)


def _gla_scan(gq, gk, gv, glog, tri_f, tri_b):
    qf, qb = _scan_specs(256)
    vf, vb = _scan_specs(512)
    gf = pl.BlockSpec((SBLK, 256), lambda b, s: (b * TILES_B + s, 0))
    gb = pl.BlockSpec((SBLK, 256), lambda b, s: (b * TILES_B + _bwd_block(s), 1))
    tri = _const_spec((SBLK, SBLK))
    return _scan_call(_gla_kernel, "gla_scan", (gq, gk, gv, glog, gq, gk, gv, glog, tri_f, tri_b),
                      [qf, qf, vf, gf, qb, qb, vb, gb, tri, tri])


def _ret_scan(l, rq, rk, rv, ld):
    qf, qb = _scan_specs(256)
    vf, vb = _scan_specs(512)
    return _scan_call(_ret_kernel, "ret_scan", (rq, rk, rv, rq, rk, rv, ld),
                      [qf, qf, vf, qb, qb, vb, _layer_spec((2, 256), l)])


def _merge_kernel(x_ref, mod_ref, n1_ref, w2_ref, oa_ref, gf_ref, gb_ref, rf_ref, rb_ref, gn_ref, rn_ref,
                  wbr_ref, wo_ref, y_ref):
    x = x_ref[...]
    sh = mod_ref[0, 0:1, :]
    sc = mod_ref[0, 1:2, :]
    gt = mod_ref[0, 2:3, :]
    hb = _mod_norm(x, n1_ref[...], sc, sh).astype(BF16)

    def gated(o, gate, ng):
        parts = []
        for h in range(4):
            sl = slice(h * LANES, (h + 1) * LANES)
            oh = o[:, sl]
            ms = jnp.mean(oh * oh, axis=-1, keepdims=True)
            parts.append((oh * lax.rsqrt(ms + EPS) * ng) * _silu(gate[:, sl]))
        return jnp.concatenate(parts, axis=1).astype(BF16)

    gates = jnp.dot(hb, w2_ref[:, 0:1024], preferred_element_type=F32)
    branches = (
        oa_ref[...],
        gated(gf_ref[...] + gb_ref[...], gates[:, 0:512], gn_ref[...]),
        gated(rf_ref[...] + rb_ref[...], gates[:, 512:1024], rn_ref[...]),
    )
    acc = jnp.zeros((TM, D_MODEL), F32)
    for z in range(N_BRANCH):
        mg = jnp.dot(hb, w2_ref[:, 1024 + z * D_MODEL:1024 + (z + 1) * D_MODEL], preferred_element_type=F32)
        acc = acc + _sigmoid(mg) * jnp.dot(branches[z], wbr_ref[z], preferred_element_type=F32)
    y = jnp.dot(acc.astype(BF16), wo_ref[...], preferred_element_type=F32)
    y_ref[...] = x + gt * y


def _merge(l, xc, mod, n1g, w2, oa, gf, gb, rf, rb, gng, rng, wbr, wo):
    row = lambda w: pl.BlockSpec((TM, w), lambda i: (i, 0))
    return pl.pallas_call(
        _merge_kernel,
        grid=(N_TILES,),
        in_specs=[
            row(D_MODEL),
            _mod_spec(l),
            _layer_spec((1, D_MODEL), l),
            _layer_spec((D_MODEL, P2_W), l),
            row(512), row(512), row(512), row(512), row(512),
            _layer_spec((1, LANES), l),
            _layer_spec((1, LANES), l),
            _layer_spec((N_BRANCH, BRANCH_W, D_MODEL), l),
            _layer_spec((D_MODEL, D_MODEL), l),
        ],
        out_specs=row(D_MODEL),
        out_shape=jax.ShapeDtypeStruct((NT, D_MODEL), F32),
        compiler_params=_params(("arbitrary",)),
        name="merge_out",
    )(xc, mod, n1g, w2, oa, gf, gb, rf, rb, gng, rng, wbr, wo)


def _ffn_kernel(x_ref, xp_ref, xn_ref, mod_ref, n2_ref, wup_ref, cw_ref, cb_ref, wdn_ref, y_ref, g_ref):
    j = pl.program_id(0) % TILES_B
    sh = mod_ref[0, 3:4, :]
    sc = mod_ref[0, 4:5, :]
    gt = mod_ref[0, 5:6, :]
    g2 = n2_ref[...]
    x = x_ref[...]
    prev_ok = j >= 2
    next_ok = jnp.logical_and(j >= 1, j <= TILES_B - 2)
    hp = jnp.where(prev_ok, _mod_norm(xp_ref[...], g2, sc, sh), 0.0)
    hn = jnp.where(next_ok, _mod_norm(xn_ref[...], g2, sc, sh), 0.0)
    h = jnp.concatenate([hp, _mod_norm(x, g2, sc, sh), hn], axis=0).astype(BF16)

    def up(col):
        return jnp.dot(h, wup_ref[:, col:col + FT], preferred_element_type=F32)

    def conv(u, col):
        w = cw_ref[:, col:col + FT]
        prev = pltpu.roll(u, 1, 0)[HALO:HALO + TM]
        nxt = pltpu.roll(u, TM + 2 * HALO - 1, 0)[HALO:HALO + TM]
        return prev * w[0:1] + u[HALO:HALO + TM] * w[1:2] + nxt * w[2:3] + cb_ref[:, col:col + FT]

    for t in range(D_FF // FT):
        a = conv(up(t * FT), t * FT)
        bv = conv(up(D_FF + t * FT), D_FF + t * FT)
        g_ref[:, t * FT:(t + 1) * FT] = (_silu(a) * bv).astype(BF16)
    y_ref[...] = x + gt * jnp.dot(g_ref[...], wdn_ref[...], preferred_element_type=F32)


def _ffn(l, xc, mod, n2g, wup, cw, cb, wdn, latent_only=False):
    per_tile = TM // HALO
    if latent_only:
        out_rows = BATCH * SEQ
        out_map = lambda i: ((i // TILES_B) * (TILES_B - 1) + jnp.maximum(i % TILES_B - 1, 0), 0)
    else:
        out_rows = NT
        out_map = lambda i: (i, 0)
    return pl.pallas_call(
        _ffn_kernel,
        grid=(N_TILES,),
        in_specs=[
            pl.BlockSpec((TM, D_MODEL), lambda i: (i, 0)),
            pl.BlockSpec((HALO, D_MODEL), lambda i: (jnp.maximum(i * per_tile - 1, 0), 0)),
            pl.BlockSpec((HALO, D_MODEL), lambda i: (jnp.minimum((i + 1) * per_tile, NT // HALO - 1), 0)),
            _mod_spec(l),
            _layer_spec((1, D_MODEL), l),
            _layer_spec((D_MODEL, 2 * D_FF), l),
            _layer_spec((3, 2 * D_FF), l),
            _layer_spec((1, 2 * D_FF), l),
            _layer_spec((D_FF, D_MODEL), l),
        ],
        out_specs=pl.BlockSpec((TM, D_MODEL), out_map),
        out_shape=jax.ShapeDtypeStruct((out_rows, D_MODEL), F32),
        scratch_shapes=[pltpu.VMEM((TM, D_FF), BF16)],
        compiler_params=_params(("arbitrary",)),
        name="conv_ffn",
    )(xc, xc, xc, mod, n2g, wup, cw, cb, wdn)


def _rope_tables():
    t = jnp.arange(SEQ)
    row = (t // GRID_W).astype(F32)
    col = (t % GRID_W).astype(F32)
    inv = ROPE_BASE ** (-jnp.arange(ROPE_FREQS, dtype=F32) * 2.0 / ROPE_AXIS_DIM)
    ang_r = row[:, None] * inv
    ang_c = col[:, None] * inv
    cos_r, sin_r, cos_c, sin_c = jnp.cos(ang_r), jnp.sin(ang_r), jnp.cos(ang_c), jnp.sin(ang_c)
    cos = jnp.concatenate([cos_r, cos_r, cos_c, cos_c], axis=1)
    sin = jnp.concatenate([-sin_r, sin_r, -sin_c, sin_c], axis=1)
    cos = jnp.concatenate([jnp.ones((CTX_LEN, HEAD_DIM), F32), cos], axis=0)
    sin = jnp.concatenate([jnp.zeros((CTX_LEN, HEAD_DIM), F32), sin], axis=0)
    return jnp.tile(cos, (1, 2)), jnp.tile(sin, (1, 2))


def _split_w_in(w_in):
    sizes = (512, 128, 128, 256, 256, 512, 512, 2 * GLA_RANK, 256, 256, 512, 512, N_BRANCH * D_MODEL)
    idx = [int(i) for i in np.cumsum(sizes)[:-1]]
    aq, ak, av, gq, gk, gv, gr, ga, rq, rk, rv, rg, mg = jnp.split(w_in, idx, axis=-1)
    pad = jnp.zeros(w_in.shape[:-1] + (LANES - 2 * GLA_RANK,), w_in.dtype)
    w1 = jnp.concatenate([aq, ak, av, gq, gk, gv, rq, rk, rv, ga, pad], axis=-1).astype(BF16)
    w2 = jnp.concatenate([gr, rg, mg], axis=-1).astype(BF16)
    return w1, w2


def kernel(x, c, ctx, c_ctx, ada_w, ada_b, norm1_g, norm2_g, w_in, attn_q_norm_g, attn_k_norm_g, attn_sink,
           gla_gate_w, gla_gate_b, gla_out_norm_g, ret_log_decay, ret_out_norm_g, w_branch, w_out, ffn_up,
           ffn_conv_w, ffn_conv_b, ffn_down):
    c_rows = jnp.concatenate([c, c_ctx[None], jnp.zeros((SUBLANES - BATCH - 1, D_MODEL), F32)], axis=0)
    mods = _ada_mods(c_rows, ada_w, ada_b).reshape(DEPTH, SUBLANES, 6, D_MODEL)
    pick = np.array([r for b in range(BATCH) for r in (BATCH, b)])
    mods = mods[:, pick]

    w1, w2 = _split_w_in(w_in)
    wbr = w_branch.astype(BF16)
    wo = w_out.astype(BF16)
    wup = ffn_up.astype(BF16)
    wdn = ffn_down.astype(BF16)
    cos_t, sin_t = _rope_tables()
    qg = jnp.tile(attn_q_norm_g, (1, 2)).reshape(DEPTH, 1, LANES)
    kg = jnp.tile(attn_k_norm_g, (1, 2)).reshape(DEPTH, 1, LANES)
    lane_group = np.arange(LANES) // HEAD_DIM
    mavg = jnp.asarray((lane_group[:, None] == lane_group[None, :]) / HEAD_DIM, BF16)
    wg = jnp.zeros((DEPTH, LANES, 512), F32)
    wg = wg.at[:, 0:GLA_RANK, 0:256].set(gla_gate_w[:, 0])
    wg = wg.at[:, GLA_RANK:2 * GLA_RANK, 256:512].set(gla_gate_w[:, 1]).astype(BF16)
    gbias = gla_gate_b.reshape(DEPTH, 1, 512)
    ld = jnp.repeat(ret_log_decay, RET_DK, axis=-1)
    rr = np.arange(SBLK)
    same = (rr[:, None] // CHUNK) == (rr[None, :] // CHUNK)
    tri_f = jnp.asarray(same & (rr[None, :] <= rr[:, None]), BF16)
    tri_b = jnp.asarray(same & (rr[None, :] >= rr[:, None]), BF16)

    n1g = norm1_g.reshape(DEPTH, 1, D_MODEL)
    n2g = norm2_g.reshape(DEPTH, 1, D_MODEL)
    gng = gla_out_norm_g.reshape(DEPTH, 1, LANES)
    rng = ret_out_norm_g.reshape(DEPTH, 1, LANES)
    cb = ffn_conv_b.reshape(DEPTH, 1, 2 * D_FF)

    xc = jnp.concatenate([ctx, x], axis=1).reshape(NT, D_MODEL)
    for l in range(DEPTH):
        qa, ka, va, gq, gk, gv, glog, rq, rk, rv = _inproj(
            l, xc, mods, n1g, w1, cos_t, sin_t, qg, kg, mavg, wg, gbias)
        oa = _attention(l, attn_sink, qa, ka, va)
        gf, gb = _gla_scan(gq, gk, gv, glog, tri_f, tri_b)
        rf, rb = _ret_scan(l, rq, rk, rv, ld)
        xc = _merge(l, xc, mods, n1g, w2, oa, gf, gb, rf, rb, gng, rng, wbr, wo)
        xc = _ffn(l, xc, mods, n2g, wup, ffn_conv_w, cb, wdn, latent_only=(l == DEPTH - 1))
    return xc.reshape(BATCH, SEQ, D_MODEL)
```

```python
import functools

import jax
import jax.numpy as jnp
import numpy as np
from jax import lax
from jax.experimental import pallas as pl
from jax.experimental.pallas import tpu as pltpu

F32 = jnp.float32
BF16 = jnp.bfloat16

D_MODEL = 1024
BATCH = 2
SEQ = 8192
DEPTH = 4
CTX_LEN = 256
GRID_W = 64
EPS = 1e-6

ATT_HEADS = 8
ATT_KV_HEADS = 2
ATT_GROUP = ATT_HEADS // ATT_KV_HEADS
HEAD_DIM = 64
WBLK = 128
ROPE_BASE = 10000.0
ROPE_AXIS_DIM = HEAD_DIM // 2
ROPE_FREQS = ROPE_AXIS_DIM // 2

GLA_HEADS = 4
GLA_DK = 64
GLA_DV = 128
GLA_RANK = 16
GLA_TAU = 16.0
RET_HEADS = 4
RET_DK = 64
RET_DV = 128
CHUNK = 64
N_BRANCH = 3
BRANCH_W = 512
D_FF = 2816
LOG2E = 1.4426950408889634

LANES = 128
SUBLANES = 8
VMEM_LIMIT = 56 * 1024 * 1024

ROWS_B = CTX_LEN + SEQ
NT = BATCH * ROWS_B
TM = 256
TILES_B = ROWS_B // TM
N_TILES = NT // TM
ABLK_B = ROWS_B // WBLK
CTX_ABLK = CTX_LEN // WBLK
SBLK = 256
CH_PER_BLK = SBLK // CHUNK
FT = 256
HALO = SUBLANES

P1_W = 512 + 128 + 128 + 256 + 256 + 512 + 256 + 256 + 512 + LANES
P2_W = 512 + 512 + N_BRANCH * D_MODEL


def _const_spec(shape):
    nd = len(shape)
    return pl.BlockSpec(shape, lambda *_: (0,) * nd, pipeline_mode=pl.Buffered(1))


def _layer_spec(shape, l):
    nd = len(shape)
    return pl.BlockSpec((None,) + tuple(shape), lambda *_: (l,) + (0,) * nd, pipeline_mode=pl.Buffered(1))


def _params(sem):
    return pltpu.CompilerParams(dimension_semantics=sem, vmem_limit_bytes=VMEM_LIMIT)


def _sigmoid(x):
    return 1.0 / (1.0 + jnp.exp(-x))


def _silu(x):
    return x * _sigmoid(x)


def _mod_norm(x, g, sc, sh):
    ms = jnp.mean(x * x, axis=-1, keepdims=True)
    return (x * lax.rsqrt(ms + EPS) * g) * (1.0 + sc) + sh


def _mod_index(i):
    return 2 * (i // TILES_B) + jnp.minimum(i % TILES_B, 1)


def _ada_kernel(c_ref, w_ref, b_ref, o_ref):
    a = _silu(c_ref[...]).astype(BF16)
    o_ref[0] = jnp.dot(a, w_ref[0].astype(BF16), preferred_element_type=F32) + b_ref[0]


def _ada_mods(c_rows, ada_w, ada_b):
    n_col = 6 * D_MODEL // 1024
    return pl.pallas_call(
        _ada_kernel,
        grid=(DEPTH, n_col),
        in_specs=[
            pl.BlockSpec((SUBLANES, D_MODEL), lambda l, j: (0, 0)),
            pl.BlockSpec((1, D_MODEL, 1024), lambda l, j: (l, 0, j)),
            pl.BlockSpec((1, 1, 1024), lambda l, j: (l, 0, j)),
        ],
        out_specs=pl.BlockSpec((1, SUBLANES, 1024), lambda l, j: (l, 0, j)),
        out_shape=jax.ShapeDtypeStruct((DEPTH, SUBLANES, 6 * D_MODEL), F32),
        compiler_params=_params(("arbitrary", "arbitrary")),
        name="ada_mods",
    )(c_rows, ada_w, ada_b.reshape(DEPTH, 1, 6 * D_MODEL))


def _head_rms(x, mavg):
    x2 = x * x
    hi = x2.astype(BF16)
    lo = (x2 - hi.astype(F32)).astype(BF16)
    return jnp.dot(hi, mavg, preferred_element_type=F32) + jnp.dot(lo, mavg, preferred_element_type=F32)


def _rope(x, cos, sin, first_half):
    up = pltpu.roll(x, LANES - ROPE_FREQS, 1)
    dn = pltpu.roll(x, ROPE_FREQS, 1)
    return x * cos + jnp.where(first_half, up, dn) * sin


def _inproj_kernel(x_ref, mod_ref, n1_ref, w_ref, cos_ref, sin_ref, qg_ref, kg_ref, mavg_ref, wg_ref, gb_ref,
                   qa_ref, ka_ref, va_ref, gq_ref, gk_ref, gv_ref, gl_ref, rq_ref, rk_ref, rv_ref):
    sh = mod_ref[0, 0:1, :]
    sc = mod_ref[0, 1:2, :]
    hb = _mod_norm(x_ref[...], n1_ref[...], sc, sh).astype(BF16)
    cos = cos_ref[...]
    sin = sin_ref[...]
    mavg = mavg_ref[...]
    lane = lax.broadcasted_iota(jnp.int32, (TM, LANES), 1)
    first_half = (lane % ROPE_AXIS_DIM) < ROPE_FREQS

    def proj(lo, hi):
        return jnp.dot(hb, w_ref[:, lo:hi], preferred_element_type=F32)

    ze = proj(2816, 2944)
    za = proj(0, 768)
    gl = jnp.dot(ze.astype(BF16), wg_ref[...], preferred_element_type=F32) + gb_ref[...]
    zc = proj(1792, 2816)
    gl_ref[...] = -(jnp.maximum(-gl, 0.0) + jnp.log1p(jnp.exp(-jnp.abs(gl)))) * (1.0 / GLA_TAU)
    qk = [za[:, cblk * LANES:(cblk + 1) * LANES] for cblk in range(5)]
    ms = [_head_rms(t, mavg) for t in qk]
    zb = proj(768, 1792)

    for cblk in range(5):
        g = qg_ref[...] if cblk < 4 else kg_ref[...]
        t = _rope(qk[cblk] * lax.rsqrt(ms[cblk] + EPS) * g, cos, sin, first_half)
        if cblk < 4:
            qa_ref[:, cblk * LANES:(cblk + 1) * LANES] = (t * (HEAD_DIM ** -0.5 * LOG2E)).astype(BF16)
        else:
            ka_ref[...] = t.astype(BF16)
    va_ref[...] = za[:, 640:768].astype(BF16)

    for cblk in range(2):
        sl = slice(cblk * LANES, (cblk + 1) * LANES)
        rq_ref[:, sl] = _rope(zc[:, sl], cos, sin, first_half).astype(BF16)
        rk = zc[:, 256 + cblk * LANES:256 + (cblk + 1) * LANES] * RET_DK ** -0.5
        rk_ref[:, sl] = _rope(rk, cos, sin, first_half).astype(BF16)
    rv_ref[...] = zc[:, 512:1024].astype(BF16)

    gq_ref[...] = (zb[:, 0:256] * GLA_DK ** -0.5).astype(BF16)
    gk_ref[...] = zb[:, 256:512].astype(BF16)
    gv_ref[...] = zb[:, 512:1024].astype(BF16)


def _mod_spec(l):
    return pl.BlockSpec((None, 1, 6, D_MODEL), lambda i: (l, _mod_index(i), 0, 0))


def _inproj(l, xc, mod, n1g, w1, cos_t, sin_t, qg, kg, mavg, wg, gbias):
    row = lambda w: pl.BlockSpec((TM, w), lambda i: (i, 0))
    widths = (512, 128, 128, 256, 256, 512, 512, 256, 256, 512)
    dtypes = (BF16, BF16, BF16, BF16, BF16, BF16, F32, BF16, BF16, BF16)
    return pl.pallas_call(
        _inproj_kernel,
        grid=(N_TILES,),
        in_specs=[
            row(D_MODEL),
            _mod_spec(l),
            _layer_spec((1, D_MODEL), l),
            _layer_spec((D_MODEL, P1_W), l),
            pl.BlockSpec((TM, LANES), lambda i: (i % TILES_B, 0)),
            pl.BlockSpec((TM, LANES), lambda i: (i % TILES_B, 0)),
            _layer_spec((1, LANES), l),
            _layer_spec((1, LANES), l),
            _const_spec((LANES, LANES)),
            _layer_spec((LANES, 512), l),
            _layer_spec((1, 512), l),
        ],
        out_specs=[row(w) for w in widths],
        out_shape=[jax.ShapeDtypeStruct((NT, w), d) for w, d in zip(widths, dtypes)],
        compiler_params=_params(("arbitrary",)),
        name="in_proj",
    )(xc, mod, n1g, w1, cos_t, sin_t, qg, kg, mavg, wg, gbias)


def _attn_kernel(l, sink_ref, band_ref, q_ref, kl_ref, ks_ref, kr_ref, kc_ref, vl_ref, vs_ref, vr_ref, vc_ref,
                 o_ref):
    r = pl.program_id(0)
    n_loc = 3 * WBLK
    n_key = n_loc + CTX_LEN
    cc = lax.broadcasted_iota(jnp.int32, (1, n_key), 1)
    kblk = r - 1 + cc // WBLK
    blk_ok = (cc >= n_loc) | ((kblk >= CTX_ABLK) & (kblk <= ABLK_B - 1) & (r >= CTX_ABLK))
    bias = band_ref[...] + jnp.where(blk_ok, 0.0, -jnp.inf)

    chains = [(b, kh) for b in range(BATCH) for kh in range(ATT_KV_HEADS)]
    k_all = [jnp.concatenate([kl_ref[b], ks_ref[b], kr_ref[b], kc_ref[b]], axis=0) for b in range(BATCH)]
    v_all = [jnp.concatenate([vl_ref[b], vs_ref[b], vr_ref[b], vc_ref[b]], axis=0) for b in range(BATCH)]
    scores = []
    for b, kh in chains:
        q = q_ref[b]
        q_s = jnp.concatenate([q[:, h * HEAD_DIM:(h + 1) * HEAD_DIM]
                               for h in range(kh * ATT_GROUP, (kh + 1) * ATT_GROUP)], axis=0)
        k_h = k_all[b][:, kh * HEAD_DIM:(kh + 1) * HEAD_DIM]
        scores.append(lax.dot_general(q_s, k_h, (((1,), (1,)), ((), ())), preferred_element_type=F32))
    outs = [[] for _ in range(BATCH)]
    for (b, kh), sc in zip(chains, scores):
        v_h = v_all[b][:, kh * HEAD_DIM:(kh + 1) * HEAD_DIM]
        sink = jnp.concatenate([jnp.full((1, WBLK, 1), sink_ref[l, h] * LOG2E, F32)
                                for h in range(kh * ATT_GROUP, (kh + 1) * ATT_GROUP)], axis=0)
        s = sc.reshape(ATT_GROUP, WBLK, n_key) + bias[None]
        m = jnp.maximum(jnp.max(s, axis=-1, keepdims=True), sink)
        e = jnp.exp2(s - m)
        denom = jnp.sum(e, axis=-1, keepdims=True) + jnp.exp2(sink - m)
        e2 = e.astype(BF16).reshape(ATT_GROUP * WBLK, n_key)
        o = jnp.dot(e2, v_h, preferred_element_type=F32).reshape(ATT_GROUP, WBLK, HEAD_DIM) * (1.0 / denom)
        outs[b].extend(o[g] for g in range(ATT_GROUP))
    for b in range(BATCH):
        o_ref[b] = jnp.concatenate(outs[b], axis=1).astype(BF16)


def _band_bias():
    ii = np.arange(WBLK)[:, None]
    cc = np.arange(3 * WBLK + CTX_LEN)[None, :]
    ok = (cc >= 3 * WBLK) | (np.abs(cc - WBLK - ii) <= WBLK)
    return jnp.asarray(np.where(ok, 0.0, -np.inf), F32)


def _attention(l, sink, qa, ka, va):
    blk = lambda f: pl.BlockSpec((BATCH, WBLK, LANES), f)
    left = lambda r: (0, jnp.maximum(r - 1, 0), 0)
    here = lambda r: (0, r, 0)
    right = lambda r: (0, jnp.minimum(r + 1, ABLK_B - 1), 0)
    ctx = pl.BlockSpec((BATCH, CTX_LEN, LANES), lambda r: (0, 0, 0))
    return pl.pallas_call(
        functools.partial(_attn_kernel, l),
        grid=(ABLK_B,),
        in_specs=[
            pl.BlockSpec(memory_space=pltpu.SMEM),
            _const_spec((WBLK, 3 * WBLK + CTX_LEN)),
            pl.BlockSpec((BATCH, WBLK, 512), here),
            blk(left), blk(here), blk(right), ctx,
            blk(left), blk(here), blk(right), ctx,
        ],
        out_specs=pl.BlockSpec((BATCH, WBLK, 512), here),
        out_shape=jax.ShapeDtypeStruct((BATCH, ROWS_B, 512), BF16),
        compiler_params=_params(("arbitrary",)),
        name="window_attn",
    )(sink, _band_bias(), qa, ka, ka, ka, ka, va, va, va, va)


def _split3(g):
    hi = g.astype(BF16)
    r1 = g - hi.astype(F32)
    mid = r1.astype(BF16)
    lo = (r1 - mid.astype(F32)).astype(BF16)
    return hi, mid, lo


_K_SL = [slice(h * GLA_DK, (h + 1) * GLA_DK) for h in range(GLA_HEADS)]
_V_SL = [slice(h * GLA_DV, (h + 1) * GLA_DV) for h in range(GLA_HEADS)]
_R_SL = [slice(c * CHUNK, (c + 1) * CHUNK) for c in range(CH_PER_BLK)]


def _scan_prep(q, k, b, tot_rows):
    width = GLA_HEADS * GLA_DK
    tot = jnp.concatenate([jnp.broadcast_to(t, (CHUNK, width)) for t in tot_rows], axis=0)
    q = q.astype(F32)
    k = k.astype(F32)
    qt = (q * jnp.exp(b)).astype(BF16)
    kt = (k * jnp.exp(-b)).astype(BF16)
    kend_t = (k * jnp.exp(tot - b)).T.astype(BF16)
    row_id = lax.broadcasted_iota(jnp.int32, (LANES, width), 0)
    tot_mat = jnp.zeros((LANES, width), F32)
    for c, t in enumerate(tot_rows):
        tot_mat = jnp.where(row_id == c, t, tot_mat)
    dec_t = jnp.exp(tot_mat.T)
    return qt, kt, kend_t, dec_t


def _scan_local(prep, v, reverse):
    qt, kt, kend_t, _ = prep
    ii = lax.broadcasted_iota(jnp.int32, (SBLK, SBLK), 0)
    jj = lax.broadcasted_iota(jnp.int32, (SBLK, SBLK), 1)
    tri = jnp.logical_and((ii // CHUNK) == (jj // CHUNK), (jj >= ii) if reverse else (jj <= ii))
    att = [lax.dot_general(qt[:, ks], kt[:, ks], (((1,), (1,)), ((), ())), preferred_element_type=F32)
           for ks in _K_SL]
    ds = [[jnp.dot(kend_t[ks, rows], v[rows, vs], preferred_element_type=F32)
           for ks, vs in zip(_K_SL, _V_SL)] for rows in _R_SL]
    intra = [jnp.dot(jnp.where(tri, a, 0.0).astype(BF16), v[:, vs], preferred_element_type=F32)
             for a, vs in zip(att, _V_SL)]
    return ds, intra


def _scan_finish(prep, local, s_ref, o_ref, reverse):
    qt, _, _, dec_t = prep
    ds, intra = local
    order = range(CH_PER_BLK - 1, -1, -1) if reverse else range(CH_PER_BLK)
    state = [s_ref[ks, :] for ks in _K_SL]
    start = [None] * CH_PER_BLK
    for c in order:
        start[c] = [s.astype(BF16) for s in state]
        state = [s * dec_t[ks, c:c + 1] + d for s, ks, d in zip(state, _K_SL, ds[c])]
    for ks, s in zip(_K_SL, state):
        s_ref[ks, :] = s
    for c, rows in enumerate(_R_SL):
        for h in range(GLA_HEADS):
            inter = jnp.dot(qt[rows, _K_SL[h]], start[c][h], preferred_element_type=F32)
            o_ref[rows, _V_SL[h]] = intra[h][rows, :] + inter


def _run_chains(chains):
    preps = [_scan_prep(q, k, b, tot) for q, k, _, b, tot, _, _, _ in chains]
    locs = [_scan_local(p, ch[2], ch[7]) for p, ch in zip(preps, chains)]
    for p, loc, ch in zip(preps, locs, chains):
        _scan_finish(p, loc, ch[5], ch[6], ch[7])


def _zero_state_at_start(sf_ref, sb_ref):
    @pl.when(pl.program_id(0) == 0)
    def _():
        sf_ref[...] = jnp.zeros_like(sf_ref)
        sb_ref[...] = jnp.zeros_like(sb_ref)


def _gla_kernel(qf_ref, kf_ref, vf_ref, gf_ref, qb_ref, kb_ref, vb_ref, gb_ref, lf_ref, lb_ref,
                of_ref, ob_ref, sf_ref, sb_ref):
    _zero_state_at_start(sf_ref, sb_ref)

    def cum(g, tri_ref):
        tri = tri_ref[...]
        return sum(jnp.dot(tri, part, preferred_element_type=F32) for part in _split3(g))

    chains = []
    for bi in range(BATCH):
        bf = cum(gf_ref[bi], lf_ref)
        tot_f = [bf[(c + 1) * CHUNK - 1:(c + 1) * CHUNK, :] for c in range(CH_PER_BLK)]
        chains.append((qf_ref[bi], kf_ref[bi], vf_ref[bi], bf, tot_f, sf_ref.at[bi], of_ref.at[bi], False))
        bb = cum(gb_ref[bi], lb_ref)
        tot_b = [bb[c * CHUNK:c * CHUNK + 1, :] for c in range(CH_PER_BLK)]
        chains.append((qb_ref[bi], kb_ref[bi], vb_ref[bi], bb, tot_b, sb_ref.at[bi], ob_ref.at[bi], True))
    _run_chains(chains)


def _ret_kernel(qf_ref, kf_ref, vf_ref, qb_ref, kb_ref, vb_ref, ld_ref, of_ref, ob_ref, sf_ref, sb_ref):
    _zero_state_at_start(sf_ref, sb_ref)
    pos = (lax.broadcasted_iota(jnp.int32, (SBLK, RET_HEADS * RET_DK), 0) % CHUNK).astype(F32)
    lg_f = -jnp.exp(ld_ref[0:1, :])
    lg_b = -jnp.exp(ld_ref[1:2, :])
    b_f = (pos + 1.0) * lg_f
    b_b = (CHUNK - pos) * lg_b
    chains = []
    for bi in range(BATCH):
        chains.append((qf_ref[bi], kf_ref[bi], vf_ref[bi], b_f, [lg_f * CHUNK] * CH_PER_BLK,
                       sf_ref.at[bi], of_ref.at[bi], False))
        chains.append((qb_ref[bi], kb_ref[bi], vb_ref[bi], b_b, [lg_b * CHUNK] * CH_PER_BLK,
                       sb_ref.at[bi], ob_ref.at[bi], True))
    _run_chains(chains)


def _bwd_block(s):
    return jnp.where(s == 0, 0, TILES_B - s)


def _scan_specs(width):
    fwd = pl.BlockSpec((BATCH, SBLK, width), lambda s: (0, s, 0))
    bwd = pl.BlockSpec((BATCH, SBLK, width), lambda s: (0, _bwd_block(s), 0))
    return fwd, bwd


def _scan_call(kernel, name, inputs, in_specs):
    of, ob = _scan_specs(512)
    return pl.pallas_call(
        kernel,
        grid=(TILES_B,),
        in_specs=in_specs,
        out_specs=[of, ob],
        out_shape=[jax.ShapeDtypeStruct((BATCH, ROWS_B, 512), F32)] * 2,
        scratch_shapes=[pltpu.VMEM((BATCH, GLA_HEADS * GLA_DK, GLA_DV), F32)] * 2,
        compiler_params=_params(("arbitrary",)),
        name=name,
    )(*inputs)


def _gla_scan(gq, gk, gv, glog, tri_f, tri_b):
    qf, qb = _scan_specs(256)
    vf, vb = _scan_specs(512)
    gf = pl.BlockSpec((BATCH, SBLK, 256), lambda s: (0, s, 0))
    gb = pl.BlockSpec((BATCH, SBLK, 256), lambda s: (0, _bwd_block(s), 1))
    tri = _const_spec((SBLK, SBLK))
    return _scan_call(_gla_kernel, "gla_scan", (gq, gk, gv, glog, gq, gk, gv, glog, tri_f, tri_b),
                      [qf, qf, vf, gf, qb, qb, vb, gb, tri, tri])


def _ret_scan(l, rq, rk, rv, ld):
    qf, qb = _scan_specs(256)
    vf, vb = _scan_specs(512)
    return _scan_call(_ret_kernel, "ret_scan", (rq, rk, rv, rq, rk, rv, ld),
                      [qf, qf, vf, qb, qb, vb, _layer_spec((2, 256), l)])


def _merge_kernel(x_ref, mod_ref, n1_ref, w2_ref, oa_ref, gf_ref, gb_ref, rf_ref, rb_ref, gn_ref, rn_ref,
                  wbr_ref, wo_ref, y_ref):
    x = x_ref[...]
    sh = mod_ref[0, 0:1, :]
    sc = mod_ref[0, 1:2, :]
    gt = mod_ref[0, 2:3, :]
    hb = _mod_norm(x, n1_ref[...], sc, sh).astype(BF16)

    def gated(o, gate, ng):
        parts = []
        for h in range(4):
            sl = slice(h * LANES, (h + 1) * LANES)
            oh = o[:, sl]
            ms = jnp.mean(oh * oh, axis=-1, keepdims=True)
            parts.append((oh * lax.rsqrt(ms + EPS) * ng) * _silu(gate[:, sl]))
        return jnp.concatenate(parts, axis=1).astype(BF16)

    gates = jnp.dot(hb, w2_ref[:, 0:1024], preferred_element_type=F32)
    mg = [jnp.dot(hb, w2_ref[:, 1024 + z * D_MODEL:1024 + (z + 1) * D_MODEL], preferred_element_type=F32)
          for z in range(N_BRANCH)]
    branches = (
        oa_ref[...],
        gated(gf_ref[...] + gb_ref[...], gates[:, 0:512], gn_ref[...]),
        gated(rf_ref[...] + rb_ref[...], gates[:, 512:1024], rn_ref[...]),
    )
    acc = jnp.zeros((TM, D_MODEL), F32)
    for z in range(N_BRANCH):
        acc = acc + _sigmoid(mg[z]) * jnp.dot(branches[z], wbr_ref[z], preferred_element_type=F32)
    y = jnp.dot(acc.astype(BF16), wo_ref[...], preferred_element_type=F32)
    y_ref[...] = x + gt * y


def _merge(l, xc, mod, n1g, w2, oa, gf, gb, rf, rb, gng, rng, wbr, wo):
    row = lambda w: pl.BlockSpec((TM, w), lambda i: (i, 0))
    return pl.pallas_call(
        _merge_kernel,
        grid=(N_TILES,),
        in_specs=[
            row(D_MODEL),
            _mod_spec(l),
            _layer_spec((1, D_MODEL), l),
            _layer_spec((D_MODEL, P2_W), l),
            row(512), row(512), row(512), row(512), row(512),
            _layer_spec((1, LANES), l),
            _layer_spec((1, LANES), l),
            _layer_spec((N_BRANCH, BRANCH_W, D_MODEL), l),
            _layer_spec((D_MODEL, D_MODEL), l),
        ],
        out_specs=row(D_MODEL),
        out_shape=jax.ShapeDtypeStruct((NT, D_MODEL), F32),
        compiler_params=_params(("arbitrary",)),
        name="merge_out",
    )(xc, mod, n1g, w2, oa, gf, gb, rf, rb, gng, rng, wbr, wo)


def _ffn_kernel(x_ref, xp_ref, xn_ref, mod_ref, n2_ref, wup_ref, cw_ref, cb_ref, wdn_ref, y_ref, g_ref):
    j = pl.program_id(0) % TILES_B
    sh = mod_ref[0, 3:4, :]
    sc = mod_ref[0, 4:5, :]
    gt = mod_ref[0, 5:6, :]
    g2 = n2_ref[...]
    x = x_ref[...]
    prev_ok = j >= 2
    next_ok = jnp.logical_and(j >= 1, j <= TILES_B - 2)
    hp = jnp.where(prev_ok, _mod_norm(xp_ref[...], g2, sc, sh), 0.0)
    hn = jnp.where(next_ok, _mod_norm(xn_ref[...], g2, sc, sh), 0.0)
    h = jnp.concatenate([hp, _mod_norm(x, g2, sc, sh), hn], axis=0).astype(BF16)

    def up(col):
        return jnp.dot(h, wup_ref[:, col:col + FT], preferred_element_type=F32)

    def conv(u, col):
        w = cw_ref[:, col:col + FT]
        prev = pltpu.roll(u, 1, 0)[HALO:HALO + TM]
        nxt = pltpu.roll(u, TM + 2 * HALO - 1, 0)[HALO:HALO + TM]
        return prev * w[0:1] + u[HALO:HALO + TM] * w[1:2] + nxt * w[2:3] + cb_ref[:, col:col + FT]

    for t in range(D_FF // FT):
        a = conv(up(t * FT), t * FT)
        bv = conv(up(D_FF + t * FT), D_FF + t * FT)
        g_ref[:, t * FT:(t + 1) * FT] = (_silu(a) * bv).astype(BF16)
    y_ref[...] = x + gt * jnp.dot(g_ref[...], wdn_ref[...], preferred_element_type=F32)


def _ffn(l, xc, mod, n2g, wup, cw, cb, wdn, latent_only=False):
    per_tile = TM // HALO
    if latent_only:
        out_rows = BATCH * SEQ
        out_map = lambda i: ((i // TILES_B) * (TILES_B - 1) + jnp.maximum(i % TILES_B - 1, 0), 0)
    else:
        out_rows = NT
        out_map = lambda i: (i, 0)
    return pl.pallas_call(
        _ffn_kernel,
        grid=(N_TILES,),
        in_specs=[
            pl.BlockSpec((TM, D_MODEL), lambda i: (i, 0)),
            pl.BlockSpec((HALO, D_MODEL), lambda i: (jnp.maximum(i * per_tile - 1, 0), 0)),
            pl.BlockSpec((HALO, D_MODEL), lambda i: (jnp.minimum((i + 1) * per_tile, NT // HALO - 1), 0)),
            _mod_spec(l),
            _layer_spec((1, D_MODEL), l),
            _layer_spec((D_MODEL, 2 * D_FF), l),
            _layer_spec((3, 2 * D_FF), l),
            _layer_spec((1, 2 * D_FF), l),
            _layer_spec((D_FF, D_MODEL), l),
        ],
        out_specs=pl.BlockSpec((TM, D_MODEL), out_map),
        out_shape=jax.ShapeDtypeStruct((out_rows, D_MODEL), F32),
        scratch_shapes=[pltpu.VMEM((TM, D_FF), BF16)],
        compiler_params=_params(("arbitrary",)),
        name="conv_ffn",
    )(xc, xc, xc, mod, n2g, wup, cw, cb, wdn)


def _rope_tables():
    t = jnp.arange(SEQ)
    row = (t // GRID_W).astype(F32)
    col = (t % GRID_W).astype(F32)
    inv = ROPE_BASE ** (-jnp.arange(ROPE_FREQS, dtype=F32) * 2.0 / ROPE_AXIS_DIM)
    ang_r = row[:, None] * inv
    ang_c = col[:, None] * inv
    cos_r, sin_r, cos_c, sin_c = jnp.cos(ang_r), jnp.sin(ang_r), jnp.cos(ang_c), jnp.sin(ang_c)
    cos = jnp.concatenate([cos_r, cos_r, cos_c, cos_c], axis=1)
    sin = jnp.concatenate([-sin_r, sin_r, -sin_c, sin_c], axis=1)
    cos = jnp.concatenate([jnp.ones((CTX_LEN, HEAD_DIM), F32), cos], axis=0)
    sin = jnp.concatenate([jnp.zeros((CTX_LEN, HEAD_DIM), F32), sin], axis=0)
    return jnp.tile(cos, (1, 2)), jnp.tile(sin, (1, 2))


def _split_w_in(w_in):
    wb = w_in.astype(BF16)
    o_gr, o_ga, o_rq, o_rg = 1792, 2304, 2304 + 2 * GLA_RANK, 2304 + 2 * GLA_RANK + 1024
    pad = jnp.zeros(w_in.shape[:-1] + (LANES - 2 * GLA_RANK,), BF16)
    w1 = jnp.concatenate([wb[..., :o_gr], wb[..., o_rq:o_rg], wb[..., o_ga:o_rq], pad], axis=-1)
    w2 = jnp.concatenate([wb[..., o_gr:o_ga], wb[..., o_rg:]], axis=-1)
    return w1, w2


def kernel(x, c, ctx, c_ctx, ada_w, ada_b, norm1_g, norm2_g, w_in, attn_q_norm_g, attn_k_norm_g, attn_sink,
           gla_gate_w, gla_gate_b, gla_out_norm_g, ret_log_decay, ret_out_norm_g, w_branch, w_out, ffn_up,
           ffn_conv_w, ffn_conv_b, ffn_down):
    c_rows = jnp.concatenate([c, c_ctx[None], jnp.zeros((SUBLANES - BATCH - 1, D_MODEL), F32)], axis=0)
    mods = _ada_mods(c_rows, ada_w, ada_b).reshape(DEPTH, SUBLANES, 6, D_MODEL)
    pick = np.array([r for b in range(BATCH) for r in (BATCH, b)])
    mods = mods[:, pick]

    w1, w2 = _split_w_in(w_in)
    wbr = w_branch.astype(BF16)
    wo = w_out.astype(BF16)
    wup = ffn_up.astype(BF16)
    wdn = ffn_down.astype(BF16)
    cos_t, sin_t = _rope_tables()
    qg = jnp.tile(attn_q_norm_g, (1, 2)).reshape(DEPTH, 1, LANES)
    kg = jnp.tile(attn_k_norm_g, (1, 2)).reshape(DEPTH, 1, LANES)
    lane_group = np.arange(LANES) // HEAD_DIM
    mavg = jnp.asarray((lane_group[:, None] == lane_group[None, :]) / HEAD_DIM, BF16)
    wg = jnp.zeros((DEPTH, LANES, 512), F32)
    wg = wg.at[:, 0:GLA_RANK, 0:256].set(gla_gate_w[:, 0])
    wg = wg.at[:, GLA_RANK:2 * GLA_RANK, 256:512].set(gla_gate_w[:, 1]).astype(BF16)
    gbias = gla_gate_b.reshape(DEPTH, 1, 512)
    ld = jnp.repeat(ret_log_decay, RET_DK, axis=-1)
    rr = np.arange(SBLK)
    same = (rr[:, None] // CHUNK) == (rr[None, :] // CHUNK)
    tri_f = jnp.asarray(same & (rr[None, :] <= rr[:, None]), BF16)
    tri_b = jnp.asarray(same & (rr[None, :] >= rr[:, None]), BF16)

    n1g = norm1_g.reshape(DEPTH, 1, D_MODEL)
    n2g = norm2_g.reshape(DEPTH, 1, D_MODEL)
    gng = gla_out_norm_g.reshape(DEPTH, 1, LANES)
    rng = ret_out_norm_g.reshape(DEPTH, 1, LANES)
    cb = ffn_conv_b.reshape(DEPTH, 1, 2 * D_FF)

    xc = jnp.concatenate([ctx, x], axis=1).reshape(NT, D_MODEL)
    for l in range(DEPTH):
        qa, ka, va, gq, gk, gv, glog, rq, rk, rv = _inproj(
            l, xc, mods, n1g, w1, cos_t, sin_t, qg, kg, mavg, wg, gbias)
        by_batch = lambda a: a.reshape(BATCH, ROWS_B, a.shape[-1])
        flat = lambda a: a.reshape(NT, a.shape[-1])
        oa = _attention(l, attn_sink, by_batch(qa), by_batch(ka), by_batch(va))
        gf, gb = _gla_scan(by_batch(gq), by_batch(gk), by_batch(gv), by_batch(glog), tri_f, tri_b)
        rf, rb = _ret_scan(l, by_batch(rq), by_batch(rk), by_batch(rv), ld)
        xc = _merge(l, xc, mods, n1g, w2, flat(oa), flat(gf), flat(gb), flat(rf), flat(rb), gng, rng, wbr, wo)
        xc = _ffn(l, xc, mods, n2g, wup, ffn_conv_w, cb, wdn, latent_only=(l == DEPTH - 1))
    return xc.reshape(BATCH, SEQ, D_MODEL)
```

```python
import functools

import jax
import jax.numpy as jnp
import numpy as np
from jax import lax
from jax.experimental import pallas as pl
from jax.experimental.pallas import tpu as pltpu

F32 = jnp.float32
BF16 = jnp.bfloat16

D_MODEL = 1024
BATCH = 2
SEQ = 8192
DEPTH = 4
CTX_LEN = 256
GRID_W = 64
EPS = 1e-6

ATT_HEADS = 8
ATT_KV_HEADS = 2
ATT_GROUP = ATT_HEADS // ATT_KV_HEADS
HEAD_DIM = 64
WBLK = 128
ROPE_BASE = 10000.0
ROPE_AXIS_DIM = HEAD_DIM // 2
ROPE_FREQS = ROPE_AXIS_DIM // 2

GLA_HEADS = 4
GLA_DK = 64
GLA_DV = 128
GLA_RANK = 16
GLA_TAU = 16.0
RET_HEADS = 4
RET_DK = 64
RET_DV = 128
CHUNK = 64
N_BRANCH = 3
BRANCH_W = 512
D_FF = 2816
LOG2E = 1.4426950408889634

LANES = 128
SUBLANES = 8
VMEM_LIMIT = 56 * 1024 * 1024

ROWS_B = CTX_LEN + SEQ
NT = BATCH * ROWS_B
TM = 256
NSUB = 3
TILES_B = ROWS_B // TM
N_TILES = NT // TM
ABLK_B = ROWS_B // WBLK
CTX_ABLK = CTX_LEN // WBLK
SBLK = 256
CH_PER_BLK = SBLK // CHUNK
FT = 256
HALO = SUBLANES

P1_W = 512 + 128 + 128 + 256 + 256 + 512 + 256 + 256 + 512 + LANES
P2_W = 512 + 512 + N_BRANCH * D_MODEL


def _const_spec(shape):
    nd = len(shape)
    return pl.BlockSpec(shape, lambda *_: (0,) * nd, pipeline_mode=pl.Buffered(1))


def _layer_spec(shape, l):
    nd = len(shape)
    return pl.BlockSpec((None,) + tuple(shape), lambda *_: (l,) + (0,) * nd, pipeline_mode=pl.Buffered(1))


def _params(sem):
    return pltpu.CompilerParams(dimension_semantics=sem, vmem_limit_bytes=VMEM_LIMIT)


def _sigmoid(x):
    return 1.0 / (1.0 + jnp.exp(-x))


def _silu(x):
    return x * _sigmoid(x)


def _mod_norm(x, g, sc, sh):
    ms = jnp.mean(x * x, axis=-1, keepdims=True)
    return (x * lax.rsqrt(ms + EPS) * g) * (1.0 + sc) + sh


def _mod_index(i):
    return 2 * (i // TILES_B) + jnp.minimum(i % TILES_B, 1)


def _ada_kernel(c_ref, w_ref, b_ref, o_ref):
    a = _silu(c_ref[...]).astype(BF16)
    o_ref[0] = jnp.dot(a, w_ref[0].astype(BF16), preferred_element_type=F32) + b_ref[0]


def _ada_mods(c_rows, ada_w, ada_b):
    n_col = 6 * D_MODEL // 1024
    return pl.pallas_call(
        _ada_kernel,
        grid=(DEPTH, n_col),
        in_specs=[
            pl.BlockSpec((SUBLANES, D_MODEL), lambda l, j: (0, 0)),
            pl.BlockSpec((1, D_MODEL, 1024), lambda l, j: (l, 0, j)),
            pl.BlockSpec((1, 1, 1024), lambda l, j: (l, 0, j)),
        ],
        out_specs=pl.BlockSpec((1, SUBLANES, 1024), lambda l, j: (l, 0, j)),
        out_shape=jax.ShapeDtypeStruct((DEPTH, SUBLANES, 6 * D_MODEL), F32),
        compiler_params=_params(("arbitrary", "arbitrary")),
        name="ada_mods",
    )(c_rows, ada_w, ada_b.reshape(DEPTH, 1, 6 * D_MODEL))


def _head_rms(x, mavg):
    x2 = x * x
    hi = x2.astype(BF16)
    lo = (x2 - hi.astype(F32)).astype(BF16)
    return jnp.dot(hi, mavg, preferred_element_type=F32) + jnp.dot(lo, mavg, preferred_element_type=F32)


def _rope(x, cos, sin, first_half):
    up = pltpu.roll(x, LANES - ROPE_FREQS, 1)
    dn = pltpu.roll(x, ROPE_FREQS, 1)
    return x * cos + jnp.where(first_half, up, dn) * sin


def _inproj_kernel(x_ref, *refs):
    mod_refs, refs = refs[:NSUB], refs[NSUB:]
    n1_ref, w_ref = refs[:2]
    cos_refs, sin_refs = refs[2:2 + NSUB], refs[2 + NSUB:2 + 2 * NSUB]
    qg_ref, kg_ref, mavg_ref, wg_ref, gb_ref = refs[2 + 2 * NSUB:7 + 2 * NSUB]
    qa_ref, ka_ref, va_ref, gq_ref, gk_ref, gv_ref, gl_ref, rq_ref, rk_ref, rv_ref = refs[7 + 2 * NSUB:]
    mavg = mavg_ref[...]
    lane = lax.broadcasted_iota(jnp.int32, (TM, LANES), 1)
    first_half = (lane % ROPE_AXIS_DIM) < ROPE_FREQS
    hbs = []
    for t in range(NSUB):
        sh = mod_refs[t][0, 0:1, :]
        sc = mod_refs[t][0, 1:2, :]
        hbs.append(_mod_norm(x_ref[t * TM:(t + 1) * TM, :], n1_ref[...], sc, sh).astype(BF16))

    for t in range(NSUB):
        hb = hbs[t]
        rows = slice(t * TM, (t + 1) * TM)
        cos = cos_refs[t][...]
        sin = sin_refs[t][...]

        def proj(lo, hi):
            return jnp.dot(hb, w_ref[:, lo:hi], preferred_element_type=F32)

        ze = proj(2816, 2944)
        za = proj(0, 768)
        gl = jnp.dot(ze.astype(BF16), wg_ref[...], preferred_element_type=F32) + gb_ref[...]
        zc = proj(1792, 2816)
        gl_ref[rows, :] = -(jnp.maximum(-gl, 0.0) + jnp.log1p(jnp.exp(-jnp.abs(gl)))) * (1.0 / GLA_TAU)
        qk = [za[:, cblk * LANES:(cblk + 1) * LANES] for cblk in range(5)]
        ms = [_head_rms(u, mavg) for u in qk]
        zb = proj(768, 1792)

        for cblk in range(5):
            g = qg_ref[...] if cblk < 4 else kg_ref[...]
            u = _rope(qk[cblk] * lax.rsqrt(ms[cblk] + EPS) * g, cos, sin, first_half)
            if cblk < 4:
                qa_ref[rows, cblk * LANES:(cblk + 1) * LANES] = (u * (HEAD_DIM ** -0.5 * LOG2E)).astype(BF16)
            else:
                ka_ref[rows, :] = u.astype(BF16)
        va_ref[:, rows] = za[:, 640:768].T.astype(BF16)

        for cblk in range(2):
            sl = slice(cblk * LANES, (cblk + 1) * LANES)
            rq_ref[rows, sl] = _rope(zc[:, sl], cos, sin, first_half).astype(BF16)
            rk = zc[:, 256 + cblk * LANES:256 + (cblk + 1) * LANES] * RET_DK ** -0.5
            rk_ref[rows, sl] = _rope(rk, cos, sin, first_half).astype(BF16)
        rv_ref[rows, :] = zc[:, 512:1024].astype(BF16)

        gq_ref[rows, :] = (zb[:, 0:256] * GLA_DK ** -0.5).astype(BF16)
        gk_ref[rows, :] = zb[:, 256:512].astype(BF16)
        gv_ref[rows, :] = zb[:, 512:1024].astype(BF16)


def _mod_spec(l):
    return pl.BlockSpec((None, 1, 6, D_MODEL), lambda i: (l, _mod_index(i), 0, 0))


def _mod_specs(l, nsub):
    return [pl.BlockSpec((None, 1, 6, D_MODEL), functools.partial(
        lambda s, t: (l, _mod_index(nsub * s + t), 0, 0), t=t)) for t in range(nsub)]


def _inproj(l, xc, mod, n1g, w1, cos_t, sin_t, qg, kg, mavg, wg, gbias):
    row = lambda w: pl.BlockSpec((NSUB * TM, w), lambda s: (s, 0))
    table = [pl.BlockSpec((TM, LANES), functools.partial(lambda s, t: ((NSUB * s + t) % TILES_B, 0), t=t))
             for t in range(NSUB)]
    widths = (512, 128, 128, 256, 256, 512, 512, 256, 256, 512)
    dtypes = (BF16, BF16, BF16, BF16, BF16, BF16, F32, BF16, BF16, BF16)
    return pl.pallas_call(
        _inproj_kernel,
        grid=(N_TILES // NSUB,),
        in_specs=[row(D_MODEL)] + _mod_specs(l, NSUB) + [
            _layer_spec((1, D_MODEL), l),
            _layer_spec((D_MODEL, P1_W), l),
        ] + table + table + [
            _layer_spec((1, LANES), l),
            _layer_spec((1, LANES), l),
            _const_spec((LANES, LANES)),
            _layer_spec((LANES, 512), l),
            _layer_spec((1, 512), l),
        ],
        out_specs=[pl.BlockSpec((None, LANES, NSUB * TM),
                                lambda s: (s // (TILES_B // NSUB), 0, s % (TILES_B // NSUB)))
                   if n == 2 else row(w) for n, w in enumerate(widths)],
        out_shape=[jax.ShapeDtypeStruct((BATCH, LANES, ROWS_B) if n == 2 else (NT, w), d)
                   for n, (w, d) in enumerate(zip(widths, dtypes))],
        compiler_params=_params(("arbitrary",)),
        name="in_proj",
    )(xc, *([mod] * NSUB), n1g, w1, *([cos_t] * NSUB), *([sin_t] * NSUB), qg, kg, mavg, wg, gbias)


def _attn_kernel(l, sink_ref, band_ref, q_ref, kl_ref, ks_ref, kr_ref, kc_ref, *refs):
    v_refs, o_ref = refs[:-1], refs[-1]
    r = pl.program_id(0)
    n_loc = 3 * WBLK
    n_key = n_loc + CTX_LEN
    n_q = ATT_GROUP * WBLK
    kk = lax.broadcasted_iota(jnp.int32, (n_key, 1), 0)
    kblk = r - 1 + kk // WBLK
    blk_ok = (kk >= n_loc) | ((kblk >= CTX_ABLK) & (kblk <= ABLK_B - 1) & (r >= CTX_ABLK))
    bias1 = band_ref[...] + jnp.where(blk_ok, 0.0, -jnp.inf)
    bias = jnp.concatenate([bias1] * ATT_GROUP, axis=1)
    head_of_lane = lax.broadcasted_iota(jnp.int32, (1, n_q), 1) // WBLK

    chains = [(b, kh) for b in range(BATCH) for kh in range(ATT_KV_HEADS)]
    k_all = [jnp.concatenate([kl_ref[b], ks_ref[b], kr_ref[b], kc_ref[b]], axis=0) for b in range(BATCH)]
    vt_all = [jnp.concatenate([v[b] for v in v_refs], axis=1) for b in range(BATCH)]
    scores = []
    for b, kh in chains:
        q = q_ref[b]
        q_s = jnp.concatenate([q[:, h * HEAD_DIM:(h + 1) * HEAD_DIM]
                               for h in range(kh * ATT_GROUP, (kh + 1) * ATT_GROUP)], axis=0)
        k_h = k_all[b][:, kh * HEAD_DIM:(kh + 1) * HEAD_DIM]
        scores.append(lax.dot_general(k_h, q_s, (((1,), (1,)), ((), ())), preferred_element_type=F32))
    for (b, kh), sc in zip(chains, scores):
        vt_h = vt_all[b][kh * HEAD_DIM:(kh + 1) * HEAD_DIM, :]
        sink = jnp.zeros((1, n_q), F32)
        for g in range(ATT_GROUP):
            sink = jnp.where(head_of_lane == g, sink_ref[l, kh * ATT_GROUP + g] * LOG2E, sink)
        s = sc + bias
        m = jnp.maximum(jnp.max(s, axis=0, keepdims=True), sink)
        e = jnp.exp2(s - m)
        denom = jnp.sum(e, axis=0, keepdims=True) + jnp.exp2(sink - m)
        o = jnp.dot(vt_h, e.astype(BF16), preferred_element_type=F32) * (1.0 / denom)
        for g in range(ATT_GROUP):
            h = kh * ATT_GROUP + g
            o_ref[b, h * HEAD_DIM:(h + 1) * HEAD_DIM, :] = o[:, g * WBLK:(g + 1) * WBLK].astype(BF16)


def _band_bias():
    kk = np.arange(3 * WBLK + CTX_LEN)[:, None]
    qq = np.arange(WBLK)[None, :]
    ok = (kk >= 3 * WBLK) | (np.abs(kk - WBLK - qq) <= WBLK)
    return jnp.asarray(np.where(ok, 0.0, -np.inf), F32)


def _attention(l, sink, qa, ka, va_t):
    blk = lambda f: pl.BlockSpec((BATCH, WBLK, LANES), f)
    left = lambda r: (0, jnp.maximum(r - 1, 0), 0)
    here = lambda r: (0, r, 0)
    right = lambda r: (0, jnp.minimum(r + 1, ABLK_B - 1), 0)
    ctx = pl.BlockSpec((BATCH, CTX_LEN, LANES), lambda r: (0, 0, 0))
    col = lambda f: (lambda r: (0, 0, f(r)[1]))
    v_specs = [pl.BlockSpec((BATCH, LANES, WBLK), col(left)), pl.BlockSpec((BATCH, LANES, WBLK), col(here)),
               pl.BlockSpec((BATCH, LANES, WBLK), col(right)),
               pl.BlockSpec((BATCH, LANES, CTX_LEN), lambda r: (0, 0, 0))]
    return pl.pallas_call(
        functools.partial(_attn_kernel, l),
        grid=(ABLK_B,),
        in_specs=[
            pl.BlockSpec(memory_space=pltpu.SMEM),
            _const_spec((3 * WBLK + CTX_LEN, WBLK)),
            pl.BlockSpec((BATCH, WBLK, 512), here),
            blk(left), blk(here), blk(right), ctx,
        ] + v_specs,
        out_specs=pl.BlockSpec((BATCH, ATT_HEADS * HEAD_DIM, WBLK), lambda r: (0, 0, r)),
        out_shape=jax.ShapeDtypeStruct((BATCH, ATT_HEADS * HEAD_DIM, ROWS_B), BF16),
        compiler_params=_params(("arbitrary",)),
        name="window_attn",
    )(sink, _band_bias(), qa, ka, ka, ka, ka, va_t, va_t, va_t, va_t)


def _split3(g):
    hi = g.astype(BF16)
    r1 = g - hi.astype(F32)
    mid = r1.astype(BF16)
    lo = (r1 - mid.astype(F32)).astype(BF16)
    return hi, mid, lo


_K_SL = [slice(h * GLA_DK, (h + 1) * GLA_DK) for h in range(GLA_HEADS)]
_V_SL = [slice(h * GLA_DV, (h + 1) * GLA_DV) for h in range(GLA_HEADS)]
_R_SL = [slice(c * CHUNK, (c + 1) * CHUNK) for c in range(CH_PER_BLK)]


def _scan_prep(q, k, b, tot_rows):
    width = GLA_HEADS * GLA_DK
    tot = jnp.concatenate([jnp.broadcast_to(t, (CHUNK, width)) for t in tot_rows], axis=0)
    q = q.astype(F32)
    k = k.astype(F32)
    qt = (q * jnp.exp(b)).astype(BF16)
    kt = (k * jnp.exp(-b)).astype(BF16)
    kend_t = (k * jnp.exp(tot - b)).T.astype(BF16)
    row_id = lax.broadcasted_iota(jnp.int32, (LANES, width), 0)
    tot_mat = jnp.zeros((LANES, width), F32)
    for c, t in enumerate(tot_rows):
        tot_mat = jnp.where(row_id == c, t, tot_mat)
    dec_t = jnp.exp(tot_mat.T)
    return qt, kt, kend_t, dec_t


def _scan_local(prep, v, reverse):
    qt, kt, kend_t, _ = prep
    ii = lax.broadcasted_iota(jnp.int32, (SBLK, SBLK), 0)
    jj = lax.broadcasted_iota(jnp.int32, (SBLK, SBLK), 1)
    tri = jnp.logical_and((ii // CHUNK) == (jj // CHUNK), (jj >= ii) if reverse else (jj <= ii))
    att = [lax.dot_general(qt[:, ks], kt[:, ks], (((1,), (1,)), ((), ())), preferred_element_type=F32)
           for ks in _K_SL]
    ds = [[jnp.dot(kend_t[ks, rows], v[rows, vs], preferred_element_type=F32)
           for ks, vs in zip(_K_SL, _V_SL)] for rows in _R_SL]
    intra = [jnp.dot(jnp.where(tri, a, 0.0).astype(BF16), v[:, vs], preferred_element_type=F32)
             for a, vs in zip(att, _V_SL)]
    return ds, intra


def _scan_finish(prep, local, s_ref, o_ref, reverse):
    qt, _, _, dec_t = prep
    ds, intra = local
    order = range(CH_PER_BLK - 1, -1, -1) if reverse else range(CH_PER_BLK)
    state = [s_ref[ks, :] for ks in _K_SL]
    start = [None] * CH_PER_BLK
    for c in order:
        start[c] = [s.astype(BF16) for s in state]
        state = [s * dec_t[ks, c:c + 1] + d for s, ks, d in zip(state, _K_SL, ds[c])]
    for ks, s in zip(_K_SL, state):
        s_ref[ks, :] = s
    for c, rows in enumerate(_R_SL):
        for h in range(GLA_HEADS):
            inter = jnp.dot(qt[rows, _K_SL[h]], start[c][h], preferred_element_type=F32)
            o_ref[rows, _V_SL[h]] = intra[h][rows, :] + inter


def _run_chains(chains):
    preps = [_scan_prep(q, k, b, tot) for q, k, _, b, tot, _, _, _ in chains]
    locs = [_scan_local(p, ch[2], ch[7]) for p, ch in zip(preps, chains)]
    for p, loc, ch in zip(preps, locs, chains):
        _scan_finish(p, loc, ch[5], ch[6], ch[7])


def _zero_state_at_start(sf_ref, sb_ref):
    @pl.when(pl.program_id(0) == 0)
    def _():
        sf_ref[...] = jnp.zeros_like(sf_ref)
        sb_ref[...] = jnp.zeros_like(sb_ref)


def _gla_kernel(qf_ref, kf_ref, vf_ref, gf_ref, qb_ref, kb_ref, vb_ref, gb_ref, lf_ref, lb_ref,
                of_ref, ob_ref, sf_ref, sb_ref):
    _zero_state_at_start(sf_ref, sb_ref)

    def cum(g, tri_ref):
        tri = tri_ref[...]
        return sum(jnp.dot(tri, part, preferred_element_type=F32) for part in _split3(g))

    chains = []
    for bi in range(BATCH):
        bf = cum(gf_ref[bi], lf_ref)
        tot_f = [bf[(c + 1) * CHUNK - 1:(c + 1) * CHUNK, :] for c in range(CH_PER_BLK)]
        chains.append((qf_ref[bi], kf_ref[bi], vf_ref[bi], bf, tot_f, sf_ref.at[bi], of_ref.at[bi], False))
        bb = cum(gb_ref[bi], lb_ref)
        tot_b = [bb[c * CHUNK:c * CHUNK + 1, :] for c in range(CH_PER_BLK)]
        chains.append((qb_ref[bi], kb_ref[bi], vb_ref[bi], bb, tot_b, sb_ref.at[bi], ob_ref.at[bi], True))
    _run_chains(chains)


def _ret_kernel(qf_ref, kf_ref, vf_ref, qb_ref, kb_ref, vb_ref, ld_ref, of_ref, ob_ref, sf_ref, sb_ref):
    _zero_state_at_start(sf_ref, sb_ref)
    pos = (lax.broadcasted_iota(jnp.int32, (SBLK, RET_HEADS * RET_DK), 0) % CHUNK).astype(F32)
    lg_f = -jnp.exp(ld_ref[0:1, :])
    lg_b = -jnp.exp(ld_ref[1:2, :])
    b_f = (pos + 1.0) * lg_f
    b_b = (CHUNK - pos) * lg_b
    chains = []
    for bi in range(BATCH):
        chains.append((qf_ref[bi], kf_ref[bi], vf_ref[bi], b_f, [lg_f * CHUNK] * CH_PER_BLK,
                       sf_ref.at[bi], of_ref.at[bi], False))
        chains.append((qb_ref[bi], kb_ref[bi], vb_ref[bi], b_b, [lg_b * CHUNK] * CH_PER_BLK,
                       sb_ref.at[bi], ob_ref.at[bi], True))
    _run_chains(chains)


def _bwd_block(s):
    return jnp.where(s == 0, 0, TILES_B - s)


def _scan_specs(width):
    fwd = pl.BlockSpec((BATCH, SBLK, width), lambda s: (0, s, 0))
    bwd = pl.BlockSpec((BATCH, SBLK, width), lambda s: (0, _bwd_block(s), 0))
    return fwd, bwd


def _scan_call(kernel, name, inputs, in_specs):
    of, ob = _scan_specs(512)
    return pl.pallas_call(
        kernel,
        grid=(TILES_B,),
        in_specs=in_specs,
        out_specs=[of, ob],
        out_shape=[jax.ShapeDtypeStruct((BATCH, ROWS_B, 512), F32)] * 2,
        scratch_shapes=[pltpu.VMEM((BATCH, GLA_HEADS * GLA_DK, GLA_DV), F32)] * 2,
        compiler_params=_params(("arbitrary",)),
        name=name,
    )(*inputs)


def _gla_scan(gq, gk, gv, glog, tri_f, tri_b):
    qf, qb = _scan_specs(256)
    vf, vb = _scan_specs(512)
    gf = pl.BlockSpec((BATCH, SBLK, 256), lambda s: (0, s, 0))
    gb = pl.BlockSpec((BATCH, SBLK, 256), lambda s: (0, _bwd_block(s), 1))
    tri = _const_spec((SBLK, SBLK))
    return _scan_call(_gla_kernel, "gla_scan", (gq, gk, gv, glog, gq, gk, gv, glog, tri_f, tri_b),
                      [qf, qf, vf, gf, qb, qb, vb, gb, tri, tri])


def _ret_scan(l, rq, rk, rv, ld):
    qf, qb = _scan_specs(256)
    vf, vb = _scan_specs(512)
    return _scan_call(_ret_kernel, "ret_scan", (rq, rk, rv, rq, rk, rv, ld),
                      [qf, qf, vf, qb, qb, vb, _layer_spec((2, 256), l)])


def _merge_kernel(x_ref, *refs):
    mod_refs, oa_refs = refs[:NSUB], refs[NSUB:2 * NSUB]
    n1_ref, w2_ref, gf_ref, gb_ref, rf_ref, rb_ref, gn_ref, rn_ref, wbr_ref, wo_ref, y_ref = refs[2 * NSUB:]

    def gated(o, gate, ng):
        parts = []
        for h in range(4):
            sl = slice(h * LANES, (h + 1) * LANES)
            oh = o[:, sl]
            ms = jnp.mean(oh * oh, axis=-1, keepdims=True)
            parts.append((oh * lax.rsqrt(ms + EPS) * ng) * _silu(gate[:, sl]))
        return jnp.concatenate(parts, axis=1).astype(BF16)

    hbs = []
    for t in range(NSUB):
        sh = mod_refs[t][0, 0:1, :]
        sc = mod_refs[t][0, 1:2, :]
        hbs.append(_mod_norm(x_ref[t * TM:(t + 1) * TM, :], n1_ref[...], sc, sh).astype(BF16))

    for t in range(NSUB):
        hb = hbs[t]
        rows = slice(t * TM, (t + 1) * TM)
        gates = jnp.dot(hb, w2_ref[:, 0:1024], preferred_element_type=F32)
        mg = [jnp.dot(hb, w2_ref[:, 1024 + z * D_MODEL:1024 + (z + 1) * D_MODEL], preferred_element_type=F32)
              for z in range(N_BRANCH)]
        ys0 = lax.dot_general(oa_refs[t][...], wbr_ref[0], (((0,), (0,)), ((), ())), preferred_element_type=F32)
        branches = (
            None,
            gated(gf_ref[rows, :] + gb_ref[rows, :], gates[:, 0:512], gn_ref[...]),
            gated(rf_ref[rows, :] + rb_ref[rows, :], gates[:, 512:1024], rn_ref[...]),
        )
        acc = _sigmoid(mg[0]) * ys0
        for z in range(1, N_BRANCH):
            acc = acc + _sigmoid(mg[z]) * jnp.dot(branches[z], wbr_ref[z], preferred_element_type=F32)
        y = jnp.dot(acc.astype(BF16), wo_ref[...], preferred_element_type=F32)
        y_ref[rows, :] = x_ref[rows, :] + mod_refs[t][0, 2:3, :] * y


def _merge(l, xc, mod, n1g, w2, oa, gf, gb, rf, rb, gng, rng, wbr, wo):
    row = lambda w: pl.BlockSpec((NSUB * TM, w), lambda s: (s, 0))
    oa_specs = [pl.BlockSpec((None, ATT_HEADS * HEAD_DIM, TM), functools.partial(
        lambda s, t: ((NSUB * s + t) // TILES_B, 0, (NSUB * s + t) % TILES_B), t=t)) for t in range(NSUB)]
    return pl.pallas_call(
        _merge_kernel,
        grid=(N_TILES // NSUB,),
        in_specs=[row(D_MODEL)] + _mod_specs(l, NSUB) + oa_specs + [
            _layer_spec((1, D_MODEL), l),
            _layer_spec((D_MODEL, P2_W), l),
            row(512), row(512), row(512), row(512),
            _layer_spec((1, LANES), l),
            _layer_spec((1, LANES), l),
            _layer_spec((N_BRANCH, BRANCH_W, D_MODEL), l),
            _layer_spec((D_MODEL, D_MODEL), l),
        ],
        out_specs=row(D_MODEL),
        out_shape=jax.ShapeDtypeStruct((NT, D_MODEL), F32),
        compiler_params=_params(("arbitrary",)),
        name="merge_out",
    )(xc, *([mod] * NSUB), *([oa] * NSUB), n1g, w2, gf, gb, rf, rb, gng, rng, wbr, wo)


def _ffn_kernel(nsub, x_ref, xp_ref, xn_ref, *refs):
    mod_refs = refs[:nsub]
    n2_ref, wup_ref, cw_ref, cb_ref, wdn_ref, y_ref, g_ref = refs[nsub:]
    g2 = n2_ref[...]
    hs = []
    for t in range(nsub):
        j = (pl.program_id(0) * nsub + t) % TILES_B
        sh = mod_refs[t][0, 3:4, :]
        sc = mod_refs[t][0, 4:5, :]
        prev_ok = j >= 2
        next_ok = jnp.logical_and(j >= 1, j <= TILES_B - 2)
        xp = xp_ref[...] if t == 0 else x_ref[t * TM - HALO:t * TM, :]
        xn = xn_ref[...] if t == nsub - 1 else x_ref[(t + 1) * TM:(t + 1) * TM + HALO, :]
        hp = jnp.where(prev_ok, _mod_norm(xp, g2, sc, sh), 0.0)
        hn = jnp.where(next_ok, _mod_norm(xn, g2, sc, sh), 0.0)
        hm = _mod_norm(x_ref[t * TM:(t + 1) * TM, :], g2, sc, sh)
        hs.append(jnp.concatenate([hp, hm, hn], axis=0).astype(BF16))

    def conv(u, col):
        w = cw_ref[:, col:col + FT]
        prev = pltpu.roll(u, 1, 0)[HALO:HALO + TM]
        nxt = pltpu.roll(u, TM + 2 * HALO - 1, 0)[HALO:HALO + TM]
        return prev * w[0:1] + u[HALO:HALO + TM] * w[1:2] + nxt * w[2:3] + cb_ref[:, col:col + FT]

    for t in range(nsub):
        h = hs[t]
        for f in range(D_FF // FT):
            a = conv(jnp.dot(h, wup_ref[:, f * FT:(f + 1) * FT], preferred_element_type=F32), f * FT)
            bv = conv(jnp.dot(h, wup_ref[:, D_FF + f * FT:D_FF + (f + 1) * FT], preferred_element_type=F32),
                      D_FF + f * FT)
            g_ref[t, :, f * FT:(f + 1) * FT] = (_silu(a) * bv).astype(BF16)
        gt = mod_refs[t][0, 5:6, :]
        rows = slice(t * TM, (t + 1) * TM)
        y_ref[rows, :] = x_ref[rows, :] + gt * jnp.dot(g_ref[t], wdn_ref[...], preferred_element_type=F32)


def _ffn(l, xc, mod, n2g, wup, cw, cb, wdn, latent_only=False):
    per_tile = TM // HALO
    if latent_only:
        nsub = 1
        out_rows = BATCH * SEQ
        out_map = lambda i: ((i // TILES_B) * (TILES_B - 1) + jnp.maximum(i % TILES_B - 1, 0), 0)
    else:
        nsub = NSUB
        out_rows = NT
        out_map = lambda s: (s, 0)
    return pl.pallas_call(
        functools.partial(_ffn_kernel, nsub),
        grid=(N_TILES // nsub,),
        in_specs=[
            pl.BlockSpec((nsub * TM, D_MODEL), lambda s: (s, 0)),
            pl.BlockSpec((HALO, D_MODEL), lambda s: (jnp.maximum(s * nsub * per_tile - 1, 0), 0)),
            pl.BlockSpec((HALO, D_MODEL),
                         lambda s: (jnp.minimum((s + 1) * nsub * per_tile, NT // HALO - 1), 0)),
        ] + _mod_specs(l, nsub) + [
            _layer_spec((1, D_MODEL), l),
            _layer_spec((D_MODEL, 2 * D_FF), l),
            _layer_spec((3, 2 * D_FF), l),
            _layer_spec((1, 2 * D_FF), l),
            _layer_spec((D_FF, D_MODEL), l),
        ],
        out_specs=pl.BlockSpec((nsub * TM, D_MODEL), out_map),
        out_shape=jax.ShapeDtypeStruct((out_rows, D_MODEL), F32),
        scratch_shapes=[pltpu.VMEM((nsub, TM, D_FF), BF16)],
        compiler_params=_params(("arbitrary",)),
        name="conv_ffn",
    )(xc, xc, xc, *([mod] * nsub), n2g, wup, cw, cb, wdn)


def _rope_tables():
    t = jnp.arange(SEQ)
    row = (t // GRID_W).astype(F32)
    col = (t % GRID_W).astype(F32)
    inv = ROPE_BASE ** (-jnp.arange(ROPE_FREQS, dtype=F32) * 2.0 / ROPE_AXIS_DIM)
    ang_r = row[:, None] * inv
    ang_c = col[:, None] * inv
    cos_r, sin_r, cos_c, sin_c = jnp.cos(ang_r), jnp.sin(ang_r), jnp.cos(ang_c), jnp.sin(ang_c)
    cos = jnp.concatenate([cos_r, cos_r, cos_c, cos_c], axis=1)
    sin = jnp.concatenate([-sin_r, sin_r, -sin_c, sin_c], axis=1)
    cos = jnp.concatenate([jnp.ones((CTX_LEN, HEAD_DIM), F32), cos], axis=0)
    sin = jnp.concatenate([jnp.zeros((CTX_LEN, HEAD_DIM), F32), sin], axis=0)
    return jnp.tile(cos, (1, 2)), jnp.tile(sin, (1, 2))


def _split_w_in(w_in):
    wb = lax.optimization_barrier(w_in.astype(BF16))
    o_gr, o_ga, o_rq, o_rg = 1792, 2304, 2304 + 2 * GLA_RANK, 2304 + 2 * GLA_RANK + 1024
    pad = jnp.zeros(w_in.shape[:-1] + (LANES - 2 * GLA_RANK,), BF16)
    w1 = jnp.concatenate([wb[..., :o_gr], wb[..., o_rq:o_rg], wb[..., o_ga:o_rq], pad], axis=-1)
    w2 = jnp.concatenate([wb[..., o_gr:o_ga], wb[..., o_rg:]], axis=-1)
    return w1, w2


def kernel(x, c, ctx, c_ctx, ada_w, ada_b, norm1_g, norm2_g, w_in, attn_q_norm_g, attn_k_norm_g, attn_sink,
           gla_gate_w, gla_gate_b, gla_out_norm_g, ret_log_decay, ret_out_norm_g, w_branch, w_out, ffn_up,
           ffn_conv_w, ffn_conv_b, ffn_down):
    c_rows = jnp.concatenate([c, c_ctx[None], jnp.zeros((SUBLANES - BATCH - 1, D_MODEL), F32)], axis=0)
    mods = _ada_mods(c_rows, ada_w, ada_b).reshape(DEPTH, SUBLANES, 6, D_MODEL)
    pick = np.array([r for b in range(BATCH) for r in (BATCH, b)])
    mods = mods[:, pick]

    w1, w2 = _split_w_in(w_in)
    wbr = w_branch.astype(BF16)
    wo = w_out.astype(BF16)
    wup = ffn_up.astype(BF16)
    wdn = ffn_down.astype(BF16)
    cos_t, sin_t = _rope_tables()
    qg = jnp.tile(attn_q_norm_g, (1, 2)).reshape(DEPTH, 1, LANES)
    kg = jnp.tile(attn_k_norm_g, (1, 2)).reshape(DEPTH, 1, LANES)
    lane_group = np.arange(LANES) // HEAD_DIM
    mavg = jnp.asarray((lane_group[:, None] == lane_group[None, :]) / HEAD_DIM, BF16)
    wg = jnp.zeros((DEPTH, LANES, 512), F32)
    wg = wg.at[:, 0:GLA_RANK, 0:256].set(gla_gate_w[:, 0])
    wg = wg.at[:, GLA_RANK:2 * GLA_RANK, 256:512].set(gla_gate_w[:, 1]).astype(BF16)
    gbias = gla_gate_b.reshape(DEPTH, 1, 512)
    ld = jnp.repeat(ret_log_decay, RET_DK, axis=-1)
    rr = np.arange(SBLK)
    same = (rr[:, None] // CHUNK) == (rr[None, :] // CHUNK)
    tri_f = jnp.asarray(same & (rr[None, :] <= rr[:, None]), BF16)
    tri_b = jnp.asarray(same & (rr[None, :] >= rr[:, None]), BF16)

    n1g = norm1_g.reshape(DEPTH, 1, D_MODEL)
    n2g = norm2_g.reshape(DEPTH, 1, D_MODEL)
    gng = gla_out_norm_g.reshape(DEPTH, 1, LANES)
    rng = ret_out_norm_g.reshape(DEPTH, 1, LANES)
    cb = ffn_conv_b.reshape(DEPTH, 1, 2 * D_FF)

    xc = jnp.concatenate([ctx, x], axis=1).reshape(NT, D_MODEL)
    for l in range(DEPTH):
        qa, ka, va, gq, gk, gv, glog, rq, rk, rv = _inproj(
            l, xc, mods, n1g, w1, cos_t, sin_t, qg, kg, mavg, wg, gbias)
        by_batch = lambda a: a.reshape(BATCH, ROWS_B, a.shape[-1])
        flat = lambda a: a.reshape(NT, a.shape[-1])
        oa = _attention(l, attn_sink, by_batch(qa), by_batch(ka), va)
        gf, gb = _gla_scan(by_batch(gq), by_batch(gk), by_batch(gv), by_batch(glog), tri_f, tri_b)
        rf, rb = _ret_scan(l, by_batch(rq), by_batch(rk), by_batch(rv), ld)
        xc = _merge(l, xc, mods, n1g, w2, oa, flat(gf), flat(gb), flat(rf), flat(rb), gng, rng, wbr, wo)
        xc = _ffn(l, xc, mods, n2g, wup, ffn_conv_w, cb, wdn, latent_only=(l == DEPTH - 1))
    return xc.reshape(BATCH, SEQ, D_MODEL)
```

```python
import functools

import jax
import jax.numpy as jnp
import numpy as np
from jax import lax
from jax.experimental import pallas as pl
from jax.experimental.pallas import tpu as pltpu

F32 = jnp.float32
BF16 = jnp.bfloat16

D_MODEL = 1024
BATCH = 2
SEQ = 8192
DEPTH = 4
CTX_LEN = 256
GRID_W = 64
EPS = 1e-6

ATT_HEADS = 8
ATT_KV_HEADS = 2
ATT_GROUP = ATT_HEADS // ATT_KV_HEADS
HEAD_DIM = 64
WBLK = 128
ROPE_BASE = 10000.0
ROPE_AXIS_DIM = HEAD_DIM // 2
ROPE_FREQS = ROPE_AXIS_DIM // 2

GLA_HEADS = 4
GLA_DK = 64
GLA_DV = 128
GLA_RANK = 16
GLA_TAU = 16.0
RET_HEADS = 4
RET_DK = 64
RET_DV = 128
CHUNK = 64
N_BRANCH = 3
BRANCH_W = 512
D_FF = 2816
LOG2E = 1.4426950408889634

LANES = 128
SUBLANES = 8
VMEM_LIMIT = 56 * 1024 * 1024

ROWS_B = CTX_LEN + SEQ
NT = BATCH * ROWS_B
TM = 256
NSUB = 3
TILES_B = ROWS_B // TM
N_TILES = NT // TM
ABLK_B = ROWS_B // WBLK
CTX_ABLK = CTX_LEN // WBLK
SBLK = 256
CH_PER_BLK = SBLK // CHUNK
FT = 256
HALO = SUBLANES

P1_W = 512 + 128 + 128 + 256 + 256 + 512 + 256 + 256 + 512 + LANES
P2_W = 512 + 512 + N_BRANCH * D_MODEL


def _const_spec(shape):
    nd = len(shape)
    return pl.BlockSpec(shape, lambda *_: (0,) * nd, pipeline_mode=pl.Buffered(1))


def _layer_spec(shape, l):
    nd = len(shape)
    return pl.BlockSpec((None,) + tuple(shape), lambda *_: (l,) + (0,) * nd, pipeline_mode=pl.Buffered(1))


def _params(sem):
    return pltpu.CompilerParams(dimension_semantics=sem, vmem_limit_bytes=VMEM_LIMIT)


def _sigmoid(x):
    return 1.0 / (1.0 + jnp.exp(-x))


def _silu(x):
    return x * _sigmoid(x)


def _mod_norm(x, g, sc, sh):
    ms = jnp.mean(x * x, axis=-1, keepdims=True)
    return (x * lax.rsqrt(ms + EPS) * g) * (1.0 + sc) + sh


def _mod_index(i):
    return 2 * (i // TILES_B) + jnp.minimum(i % TILES_B, 1)


def _ada_kernel(c_ref, w_ref, b_ref, o_ref):
    a = _silu(c_ref[...]).astype(BF16)
    o_ref[0] = jnp.dot(a, w_ref[0].astype(BF16), preferred_element_type=F32) + b_ref[0]


def _ada_mods(c_rows, ada_w, ada_b):
    n_col = 6 * D_MODEL // 1024
    return pl.pallas_call(
        _ada_kernel,
        grid=(DEPTH, n_col),
        in_specs=[
            pl.BlockSpec((SUBLANES, D_MODEL), lambda l, j: (0, 0)),
            pl.BlockSpec((1, D_MODEL, 1024), lambda l, j: (l, 0, j)),
            pl.BlockSpec((1, 1, 1024), lambda l, j: (l, 0, j)),
        ],
        out_specs=pl.BlockSpec((1, SUBLANES, 1024), lambda l, j: (l, 0, j)),
        out_shape=jax.ShapeDtypeStruct((DEPTH, SUBLANES, 6 * D_MODEL), F32),
        compiler_params=_params(("arbitrary", "arbitrary")),
        name="ada_mods",
    )(c_rows, ada_w, ada_b.reshape(DEPTH, 1, 6 * D_MODEL))


def _head_rms(x, mavg):
    x2 = x * x
    hi = x2.astype(BF16)
    lo = (x2 - hi.astype(F32)).astype(BF16)
    return jnp.dot(hi, mavg, preferred_element_type=F32) + jnp.dot(lo, mavg, preferred_element_type=F32)


def _rope(x, cos, sin, first_half):
    up = pltpu.roll(x, LANES - ROPE_FREQS, 1)
    dn = pltpu.roll(x, ROPE_FREQS, 1)
    return x * cos + jnp.where(first_half, up, dn) * sin


def _row_tiles(split, refs):
    if not split:
        return [refs[0][t * TM:(t + 1) * TM, :] for t in range(NSUB)], refs[1:]
    is_ctx = pl.program_id(0) % (TILES_B // NSUB) == 0
    tiles = [r[...] for r in refs[:NSUB]]
    tiles[0] = jnp.where(is_ctx, refs[NSUB][...], tiles[0])
    return tiles, refs[NSUB + 1:]


def _row_tile_specs(split):
    if not split:
        return [pl.BlockSpec((NSUB * TM, D_MODEL), lambda s: (s, 0))]
    per_b = TILES_B // NSUB

    def latent(s, t):
        return ((s // per_b) * (TILES_B - 1) + jnp.maximum(NSUB * (s % per_b) + t - 1, 0), 0)

    return ([pl.BlockSpec((TM, D_MODEL), functools.partial(latent, t=t)) for t in range(NSUB)]
            + [pl.BlockSpec((TM, D_MODEL), lambda s: (s // per_b, 0))])


def _inproj_kernel(split, *refs):
    x_tiles, refs = _row_tiles(split, refs)
    mod_refs, refs = refs[:NSUB], refs[NSUB:]
    n1_ref, w_ref = refs[:2]
    cos_refs, sin_refs = refs[2:2 + NSUB], refs[2 + NSUB:2 + 2 * NSUB]
    qg_ref, kg_ref, mavg_ref, wg_ref, gb_ref = refs[2 + 2 * NSUB:7 + 2 * NSUB]
    qa_ref, ka_ref, va_ref, gq_ref, gk_ref, gv_ref, gl_ref, rq_ref, rk_ref, rv_ref = refs[7 + 2 * NSUB:]
    mavg = mavg_ref[...]
    lane = lax.broadcasted_iota(jnp.int32, (TM, LANES), 1)
    first_half = (lane % ROPE_AXIS_DIM) < ROPE_FREQS
    hbs = []
    for t in range(NSUB):
        sh = mod_refs[t][0, 0:1, :]
        sc = mod_refs[t][0, 1:2, :]
        hbs.append(_mod_norm(x_tiles[t], n1_ref[...], sc, sh).astype(BF16))

    for t in range(NSUB):
        hb = hbs[t]
        rows = slice(t * TM, (t + 1) * TM)
        cos = cos_refs[t][...]
        sin = sin_refs[t][...]

        def proj(lo, hi):
            return jnp.dot(hb, w_ref[:, lo:hi], preferred_element_type=F32)

        ze = proj(2816, 2944)
        za = proj(0, 768)
        gl = jnp.dot(ze.astype(BF16), wg_ref[...], preferred_element_type=F32) + gb_ref[...]
        zc = proj(1792, 2816)
        gl_ref[rows, :] = -(jnp.maximum(-gl, 0.0) + jnp.log1p(jnp.exp(-jnp.abs(gl)))) * (1.0 / GLA_TAU)
        qk = [za[:, cblk * LANES:(cblk + 1) * LANES] for cblk in range(5)]
        ms = [_head_rms(u, mavg) for u in qk]
        zb = proj(768, 1792)

        for cblk in range(5):
            g = qg_ref[...] if cblk < 4 else kg_ref[...]
            u = _rope(qk[cblk] * lax.rsqrt(ms[cblk] + EPS) * g, cos, sin, first_half)
            if cblk < 4:
                qa_ref[rows, cblk * LANES:(cblk + 1) * LANES] = (u * (HEAD_DIM ** -0.5 * LOG2E)).astype(BF16)
            else:
                ka_ref[rows, :] = u.astype(BF16)
        va_ref[:, rows] = za[:, 640:768].T.astype(BF16)

        for cblk in range(2):
            sl = slice(cblk * LANES, (cblk + 1) * LANES)
            rq_ref[rows, sl] = _rope(zc[:, sl], cos, sin, first_half).astype(BF16)
            rk = zc[:, 256 + cblk * LANES:256 + (cblk + 1) * LANES] * RET_DK ** -0.5
            rk_ref[rows, sl] = _rope(rk, cos, sin, first_half).astype(BF16)
        rv_ref[rows, :] = zc[:, 512:1024].astype(BF16)

        gq_ref[rows, :] = (zb[:, 0:256] * GLA_DK ** -0.5).astype(BF16)
        gk_ref[rows, :] = zb[:, 256:512].astype(BF16)
        gv_ref[rows, :] = zb[:, 512:1024].astype(BF16)


def _mod_spec(l):
    return pl.BlockSpec((None, 1, 6, D_MODEL), lambda i: (l, _mod_index(i), 0, 0))


def _mod_specs(l, nsub):
    return [pl.BlockSpec((None, 1, 6, D_MODEL), functools.partial(
        lambda s, t: (l, _mod_index(nsub * s + t), 0, 0), t=t)) for t in range(nsub)]


def _inproj(l, xs, mod, n1g, w1, cos_t, sin_t, qg, kg, mavg, wg, gbias):
    split = len(xs) == 2
    stream = [xs[0]] * NSUB + [xs[1]] if split else list(xs)
    row = lambda w: pl.BlockSpec((NSUB * TM, w), lambda s: (s, 0))
    table = [pl.BlockSpec((TM, LANES), functools.partial(lambda s, t: ((NSUB * s + t) % TILES_B, 0), t=t))
             for t in range(NSUB)]
    widths = (512, 128, 128, 256, 256, 512, 512, 256, 256, 512)
    dtypes = (BF16, BF16, BF16, BF16, BF16, BF16, F32, BF16, BF16, BF16)
    return pl.pallas_call(
        functools.partial(_inproj_kernel, split),
        grid=(N_TILES // NSUB,),
        in_specs=_row_tile_specs(split) + _mod_specs(l, NSUB) + [
            _layer_spec((1, D_MODEL), l),
            _layer_spec((D_MODEL, P1_W), l),
        ] + table + table + [
            _layer_spec((1, LANES), l),
            _layer_spec((1, LANES), l),
            _const_spec((LANES, LANES)),
            _layer_spec((LANES, 512), l),
            _layer_spec((1, 512), l),
        ],
        out_specs=[pl.BlockSpec((None, LANES, NSUB * TM),
                                lambda s: (s // (TILES_B // NSUB), 0, s % (TILES_B // NSUB)))
                   if n == 2 else row(w) for n, w in enumerate(widths)],
        out_shape=[jax.ShapeDtypeStruct((BATCH, LANES, ROWS_B) if n == 2 else (NT, w), d)
                   for n, (w, d) in enumerate(zip(widths, dtypes))],
        compiler_params=_params(("arbitrary",)),
        name="in_proj",
    )(*stream, *([mod] * NSUB), n1g, w1, *([cos_t] * NSUB), *([sin_t] * NSUB), qg, kg, mavg, wg, gbias)


def _attn_kernel(l, sink_ref, band_ref, q_ref, kl_ref, ks_ref, kr_ref, kc_ref, *refs):
    v_refs, o_ref = refs[:-1], refs[-1]
    r = pl.program_id(0)
    n_loc = 3 * WBLK
    n_key = n_loc + CTX_LEN
    n_q = ATT_GROUP * WBLK
    kk = lax.broadcasted_iota(jnp.int32, (n_key, 1), 0)
    kblk = r - 1 + kk // WBLK
    blk_ok = (kk >= n_loc) | ((kblk >= CTX_ABLK) & (kblk <= ABLK_B - 1) & (r >= CTX_ABLK))
    bias1 = band_ref[...] + jnp.where(blk_ok, 0.0, -jnp.inf)
    bias = jnp.concatenate([bias1] * ATT_GROUP, axis=1)
    head_of_lane = lax.broadcasted_iota(jnp.int32, (1, n_q), 1) // WBLK

    chains = [(b, kh) for b in range(BATCH) for kh in range(ATT_KV_HEADS)]
    k_all = [jnp.concatenate([kl_ref[b], ks_ref[b], kr_ref[b], kc_ref[b]], axis=0) for b in range(BATCH)]
    vt_all = [jnp.concatenate([v[b] for v in v_refs], axis=1) for b in range(BATCH)]
    scores = []
    for b, kh in chains:
        q = q_ref[b]
        q_s = jnp.concatenate([q[:, h * HEAD_DIM:(h + 1) * HEAD_DIM]
                               for h in range(kh * ATT_GROUP, (kh + 1) * ATT_GROUP)], axis=0)
        k_h = k_all[b][:, kh * HEAD_DIM:(kh + 1) * HEAD_DIM]
        scores.append(lax.dot_general(k_h, q_s, (((1,), (1,)), ((), ())), preferred_element_type=F32))
    for (b, kh), sc in zip(chains, scores):
        vt_h = vt_all[b][kh * HEAD_DIM:(kh + 1) * HEAD_DIM, :]
        sink = jnp.zeros((1, n_q), F32)
        for g in range(ATT_GROUP):
            sink = jnp.where(head_of_lane == g, sink_ref[l, kh * ATT_GROUP + g] * LOG2E, sink)
        s = sc + bias
        m = jnp.maximum(jnp.max(s, axis=0, keepdims=True), sink)
        e = jnp.exp2(s - m)
        denom = jnp.sum(e, axis=0, keepdims=True) + jnp.exp2(sink - m)
        o = jnp.dot(vt_h, e.astype(BF16), preferred_element_type=F32) * (1.0 / denom)
        for g in range(ATT_GROUP):
            h = kh * ATT_GROUP + g
            o_ref[b, h * HEAD_DIM:(h + 1) * HEAD_DIM, :] = o[:, g * WBLK:(g + 1) * WBLK].astype(BF16)


def _band_bias():
    kk = np.arange(3 * WBLK + CTX_LEN)[:, None]
    qq = np.arange(WBLK)[None, :]
    ok = (kk >= 3 * WBLK) | (np.abs(kk - WBLK - qq) <= WBLK)
    return jnp.asarray(np.where(ok, 0.0, -np.inf), F32)


def _attention(l, sink, qa, ka, va_t):
    blk = lambda f: pl.BlockSpec((BATCH, WBLK, LANES), f)
    left = lambda r: (0, jnp.maximum(r - 1, 0), 0)
    here = lambda r: (0, r, 0)
    right = lambda r: (0, jnp.minimum(r + 1, ABLK_B - 1), 0)
    ctx = pl.BlockSpec((BATCH, CTX_LEN, LANES), lambda r: (0, 0, 0))
    col = lambda f: (lambda r: (0, 0, f(r)[1]))
    v_specs = [pl.BlockSpec((BATCH, LANES, WBLK), col(left)), pl.BlockSpec((BATCH, LANES, WBLK), col(here)),
               pl.BlockSpec((BATCH, LANES, WBLK), col(right)),
               pl.BlockSpec((BATCH, LANES, CTX_LEN), lambda r: (0, 0, 0))]
    return pl.pallas_call(
        functools.partial(_attn_kernel, l),
        grid=(ABLK_B,),
        in_specs=[
            pl.BlockSpec(memory_space=pltpu.SMEM),
            _const_spec((3 * WBLK + CTX_LEN, WBLK)),
            pl.BlockSpec((BATCH, WBLK, 512), here),
            blk(left), blk(here), blk(right), ctx,
        ] + v_specs,
        out_specs=pl.BlockSpec((BATCH, ATT_HEADS * HEAD_DIM, WBLK), lambda r: (0, 0, r)),
        out_shape=jax.ShapeDtypeStruct((BATCH, ATT_HEADS * HEAD_DIM, ROWS_B), BF16),
        compiler_params=_params(("arbitrary",)),
        name="window_attn",
    )(sink, _band_bias(), qa, ka, ka, ka, ka, va_t, va_t, va_t, va_t)


def _split3(g):
    hi = g.astype(BF16)
    r1 = g - hi.astype(F32)
    mid = r1.astype(BF16)
    lo = (r1 - mid.astype(F32)).astype(BF16)
    return hi, mid, lo


_K_SL = [slice(h * GLA_DK, (h + 1) * GLA_DK) for h in range(GLA_HEADS)]
_V_SL = [slice(h * GLA_DV, (h + 1) * GLA_DV) for h in range(GLA_HEADS)]
_R_SL = [slice(c * CHUNK, (c + 1) * CHUNK) for c in range(CH_PER_BLK)]


def _scan_prep(q, k, b, tot_rows):
    width = GLA_HEADS * GLA_DK
    tot = jnp.concatenate([jnp.broadcast_to(t, (CHUNK, width)) for t in tot_rows], axis=0)
    q = q.astype(F32)
    k = k.astype(F32)
    qt = (q * jnp.exp(b)).astype(BF16)
    kt = (k * jnp.exp(-b)).astype(BF16)
    kend_t = (k * jnp.exp(tot - b)).T.astype(BF16)
    row_id = lax.broadcasted_iota(jnp.int32, (LANES, width), 0)
    tot_mat = jnp.zeros((LANES, width), F32)
    for c, t in enumerate(tot_rows):
        tot_mat = jnp.where(row_id == c, t, tot_mat)
    dec_t = jnp.exp(tot_mat.T)
    return qt, kt, kend_t, dec_t


def _scan_local(prep, v, reverse):
    qt, kt, kend_t, _ = prep
    ii = lax.broadcasted_iota(jnp.int32, (SBLK, SBLK), 0)
    jj = lax.broadcasted_iota(jnp.int32, (SBLK, SBLK), 1)
    tri = jnp.logical_and((ii // CHUNK) == (jj // CHUNK), (jj >= ii) if reverse else (jj <= ii))
    att = [lax.dot_general(qt[:, ks], kt[:, ks], (((1,), (1,)), ((), ())), preferred_element_type=F32)
           for ks in _K_SL]
    ds = [[jnp.dot(kend_t[ks, rows], v[rows, vs], preferred_element_type=F32)
           for ks, vs in zip(_K_SL, _V_SL)] for rows in _R_SL]
    intra = [jnp.dot(jnp.where(tri, a, 0.0).astype(BF16), v[:, vs], preferred_element_type=F32)
             for a, vs in zip(att, _V_SL)]
    return ds, intra


def _scan_finish(prep, local, s_ref, o_ref, reverse):
    qt, _, _, dec_t = prep
    ds, intra = local
    order = range(CH_PER_BLK - 1, -1, -1) if reverse else range(CH_PER_BLK)
    state = [s_ref[ks, :] for ks in _K_SL]
    start = [None] * CH_PER_BLK
    for c in order:
        start[c] = [s.astype(BF16) for s in state]
        state = [s * dec_t[ks, c:c + 1] + d for s, ks, d in zip(state, _K_SL, ds[c])]
    for ks, s in zip(_K_SL, state):
        s_ref[ks, :] = s
    for c, rows in enumerate(_R_SL):
        for h in range(GLA_HEADS):
            inter = jnp.dot(qt[rows, _K_SL[h]], start[c][h], preferred_element_type=F32)
            o_ref[rows, _V_SL[h]] = intra[h][rows, :] + inter


def _run_chains(chains):
    preps = [_scan_prep(q, k, b, tot) for q, k, _, b, tot, _, _, _ in chains]
    locs = [_scan_local(p, ch[2], ch[7]) for p, ch in zip(preps, chains)]
    for p, loc, ch in zip(preps, locs, chains):
        _scan_finish(p, loc, ch[5], ch[6], ch[7])


def _scan_kernel(gqf_ref, gkf_ref, gvf_ref, ggf_ref, gqb_ref, gkb_ref, gvb_ref, ggb_ref, lf_ref, lb_ref,
                 rqf_ref, rkf_ref, rvf_ref, rqb_ref, rkb_ref, rvb_ref, ld_ref,
                 gof_ref, gob_ref, rof_ref, rob_ref, gsf_ref, gsb_ref, rsf_ref, rsb_ref):
    @pl.when(pl.program_id(0) == 0)
    def _():
        for s_ref in (gsf_ref, gsb_ref, rsf_ref, rsb_ref):
            s_ref[...] = jnp.zeros_like(s_ref)

    def cum(g, tri_ref):
        tri = tri_ref[...]
        return sum(jnp.dot(tri, part, preferred_element_type=F32) for part in _split3(g))

    pos = (lax.broadcasted_iota(jnp.int32, (SBLK, RET_HEADS * RET_DK), 0) % CHUNK).astype(F32)
    lg_f = -jnp.exp(ld_ref[0:1, :])
    lg_b = -jnp.exp(ld_ref[1:2, :])
    b_f = (pos + 1.0) * lg_f
    b_b = (CHUNK - pos) * lg_b
    chains = []
    for bi in range(BATCH):
        bf = cum(ggf_ref[bi], lf_ref)
        tot_f = [bf[(c + 1) * CHUNK - 1:(c + 1) * CHUNK, :] for c in range(CH_PER_BLK)]
        chains.append((gqf_ref[bi], gkf_ref[bi], gvf_ref[bi], bf, tot_f, gsf_ref.at[bi], gof_ref.at[bi], False))
        bb = cum(ggb_ref[bi], lb_ref)
        tot_b = [bb[c * CHUNK:c * CHUNK + 1, :] for c in range(CH_PER_BLK)]
        chains.append((gqb_ref[bi], gkb_ref[bi], gvb_ref[bi], bb, tot_b, gsb_ref.at[bi], gob_ref.at[bi], True))
        chains.append((rqf_ref[bi], rkf_ref[bi], rvf_ref[bi], b_f, [lg_f * CHUNK] * CH_PER_BLK,
                       rsf_ref.at[bi], rof_ref.at[bi], False))
        chains.append((rqb_ref[bi], rkb_ref[bi], rvb_ref[bi], b_b, [lg_b * CHUNK] * CH_PER_BLK,
                       rsb_ref.at[bi], rob_ref.at[bi], True))
    _run_chains(chains)


def _bwd_block(s):
    return jnp.where(s == 0, 0, TILES_B - s)


def _scan_specs(width):
    fwd = pl.BlockSpec((BATCH, SBLK, width), lambda s: (0, s, 0))
    bwd = pl.BlockSpec((BATCH, SBLK, width), lambda s: (0, _bwd_block(s), 0))
    return fwd, bwd


def _scans(l, gq, gk, gv, glog, tri_f, tri_b, rq, rk, rv, ld):
    qf, qb = _scan_specs(256)
    vf, vb = _scan_specs(512)
    gf = pl.BlockSpec((BATCH, SBLK, 256), lambda s: (0, s, 0))
    gb = pl.BlockSpec((BATCH, SBLK, 256), lambda s: (0, _bwd_block(s), 1))
    tri = _const_spec((SBLK, SBLK))
    return pl.pallas_call(
        _scan_kernel,
        grid=(TILES_B,),
        in_specs=[qf, qf, vf, gf, qb, qb, vb, gb, tri, tri, qf, qf, vf, qb, qb, vb, _layer_spec((2, 256), l)],
        out_specs=[vf, vb, vf, vb],
        out_shape=[jax.ShapeDtypeStruct((BATCH, ROWS_B, 512), F32)] * 4,
        scratch_shapes=[pltpu.VMEM((BATCH, GLA_HEADS * GLA_DK, GLA_DV), F32)] * 4,
        compiler_params=_params(("arbitrary",)),
        name="bidir_scans",
    )(gq, gk, gv, glog, gq, gk, gv, glog, tri_f, tri_b, rq, rk, rv, rq, rk, rv, ld)


def _merge_kernel(split, *refs):
    x_tiles, refs = _row_tiles(split, refs)
    mod_refs, oa_refs = refs[:NSUB], refs[NSUB:2 * NSUB]
    n1_ref, w2_ref, gf_ref, gb_ref, rf_ref, rb_ref, gn_ref, rn_ref, wbr_ref, wo_ref, y_ref = refs[2 * NSUB:]

    def gated(o, gate, ng):
        parts = []
        for h in range(4):
            sl = slice(h * LANES, (h + 1) * LANES)
            oh = o[:, sl]
            ms = jnp.mean(oh * oh, axis=-1, keepdims=True)
            parts.append((oh * lax.rsqrt(ms + EPS) * ng) * _silu(gate[:, sl]))
        return jnp.concatenate(parts, axis=1).astype(BF16)

    hbs = []
    for t in range(NSUB):
        sh = mod_refs[t][0, 0:1, :]
        sc = mod_refs[t][0, 1:2, :]
        hbs.append(_mod_norm(x_tiles[t], n1_ref[...], sc, sh).astype(BF16))

    for t in range(NSUB):
        hb = hbs[t]
        rows = slice(t * TM, (t + 1) * TM)
        gates = jnp.dot(hb, w2_ref[:, 0:1024], preferred_element_type=F32)
        mg = [jnp.dot(hb, w2_ref[:, 1024 + z * D_MODEL:1024 + (z + 1) * D_MODEL], preferred_element_type=F32)
              for z in range(N_BRANCH)]
        ys0 = lax.dot_general(oa_refs[t][...], wbr_ref[0], (((0,), (0,)), ((), ())), preferred_element_type=F32)
        branches = (
            None,
            gated(gf_ref[rows, :] + gb_ref[rows, :], gates[:, 0:512], gn_ref[...]),
            gated(rf_ref[rows, :] + rb_ref[rows, :], gates[:, 512:1024], rn_ref[...]),
        )
        acc = _sigmoid(mg[0]) * ys0
        for z in range(1, N_BRANCH):
            acc = acc + _sigmoid(mg[z]) * jnp.dot(branches[z], wbr_ref[z], preferred_element_type=F32)
        y = jnp.dot(acc.astype(BF16), wo_ref[...], preferred_element_type=F32)
        y_ref[rows, :] = x_tiles[t] + mod_refs[t][0, 2:3, :] * y


def _merge(l, xs, mod, n1g, w2, oa, gf, gb, rf, rb, gng, rng, wbr, wo):
    split = len(xs) == 2
    stream = [xs[0]] * NSUB + [xs[1]] if split else list(xs)
    row = lambda w: pl.BlockSpec((NSUB * TM, w), lambda s: (s, 0))
    oa_specs = [pl.BlockSpec((None, ATT_HEADS * HEAD_DIM, TM), functools.partial(
        lambda s, t: ((NSUB * s + t) // TILES_B, 0, (NSUB * s + t) % TILES_B), t=t)) for t in range(NSUB)]
    return pl.pallas_call(
        functools.partial(_merge_kernel, split),
        grid=(N_TILES // NSUB,),
        in_specs=_row_tile_specs(split) + _mod_specs(l, NSUB) + oa_specs + [
            _layer_spec((1, D_MODEL), l),
            _layer_spec((D_MODEL, P2_W), l),
            row(512), row(512), row(512), row(512),
            _layer_spec((1, LANES), l),
            _layer_spec((1, LANES), l),
            _layer_spec((N_BRANCH, BRANCH_W, D_MODEL), l),
            _layer_spec((D_MODEL, D_MODEL), l),
        ],
        out_specs=row(D_MODEL),
        out_shape=jax.ShapeDtypeStruct((NT, D_MODEL), F32),
        compiler_params=_params(("arbitrary",)),
        name="merge_out",
    )(*stream, *([mod] * NSUB), *([oa] * NSUB), n1g, w2, gf, gb, rf, rb, gng, rng, wbr, wo)


def _ffn_kernel(nsub, x_ref, xp_ref, xn_ref, *refs):
    mod_refs = refs[:nsub]
    n2_ref, wup_ref, cw_ref, cb_ref, wdn_ref, y_ref, g_ref = refs[nsub:]
    g2 = n2_ref[...]
    hs = []
    for t in range(nsub):
        j = (pl.program_id(0) * nsub + t) % TILES_B
        sh = mod_refs[t][0, 3:4, :]
        sc = mod_refs[t][0, 4:5, :]
        prev_ok = j >= 2
        next_ok = jnp.logical_and(j >= 1, j <= TILES_B - 2)
        xp = xp_ref[...] if t == 0 else x_ref[t * TM - HALO:t * TM, :]
        xn = xn_ref[...] if t == nsub - 1 else x_ref[(t + 1) * TM:(t + 1) * TM + HALO, :]
        hp = jnp.where(prev_ok, _mod_norm(xp, g2, sc, sh), 0.0)
        hn = jnp.where(next_ok, _mod_norm(xn, g2, sc, sh), 0.0)
        hm = _mod_norm(x_ref[t * TM:(t + 1) * TM, :], g2, sc, sh)
        hs.append(jnp.concatenate([hp, hm, hn], axis=0).astype(BF16))

    def conv(u, col):
        w = cw_ref[:, col:col + FT]
        prev = pltpu.roll(u, 1, 0)[HALO:HALO + TM]
        nxt = pltpu.roll(u, TM + 2 * HALO - 1, 0)[HALO:HALO + TM]
        return prev * w[0:1] + u[HALO:HALO + TM] * w[1:2] + nxt * w[2:3] + cb_ref[:, col:col + FT]

    for t in range(nsub):
        h = hs[t]
        for f in range(D_FF // FT):
            a = conv(jnp.dot(h, wup_ref[:, f * FT:(f + 1) * FT], preferred_element_type=F32), f * FT)
            bv = conv(jnp.dot(h, wup_ref[:, D_FF + f * FT:D_FF + (f + 1) * FT], preferred_element_type=F32),
                      D_FF + f * FT)
            g_ref[t, :, f * FT:(f + 1) * FT] = (_silu(a) * bv).astype(BF16)
        gt = mod_refs[t][0, 5:6, :]
        rows = slice(t * TM, (t + 1) * TM)
        y_ref[rows, :] = x_ref[rows, :] + gt * jnp.dot(g_ref[t], wdn_ref[...], preferred_element_type=F32)


def _ffn(l, xc, mod, n2g, wup, cw, cb, wdn, latent_only=False):
    per_tile = TM // HALO
    if latent_only:
        nsub = 1
        out_rows = BATCH * SEQ
        out_map = lambda i: ((i // TILES_B) * (TILES_B - 1) + jnp.maximum(i % TILES_B - 1, 0), 0)
    else:
        nsub = NSUB
        out_rows = NT
        out_map = lambda s: (s, 0)
    return pl.pallas_call(
        functools.partial(_ffn_kernel, nsub),
        grid=(N_TILES // nsub,),
        in_specs=[
            pl.BlockSpec((nsub * TM, D_MODEL), lambda s: (s, 0)),
            pl.BlockSpec((HALO, D_MODEL), lambda s: (jnp.maximum(s * nsub * per_tile - 1, 0), 0)),
            pl.BlockSpec((HALO, D_MODEL),
                         lambda s: (jnp.minimum((s + 1) * nsub * per_tile, NT // HALO - 1), 0)),
        ] + _mod_specs(l, nsub) + [
            _layer_spec((1, D_MODEL), l),
            _layer_spec((D_MODEL, 2 * D_FF), l),
            _layer_spec((3, 2 * D_FF), l),
            _layer_spec((1, 2 * D_FF), l),
            _layer_spec((D_FF, D_MODEL), l),
        ],
        out_specs=pl.BlockSpec((nsub * TM, D_MODEL), out_map),
        out_shape=jax.ShapeDtypeStruct((out_rows, D_MODEL), F32),
        scratch_shapes=[pltpu.VMEM((nsub, TM, D_FF), BF16)],
        compiler_params=_params(("arbitrary",)),
        name="conv_ffn",
    )(xc, xc, xc, *([mod] * nsub), n2g, wup, cw, cb, wdn)


def _rope_tables():
    t = jnp.arange(SEQ)
    row = (t // GRID_W).astype(F32)
    col = (t % GRID_W).astype(F32)
    inv = ROPE_BASE ** (-jnp.arange(ROPE_FREQS, dtype=F32) * 2.0 / ROPE_AXIS_DIM)
    ang_r = row[:, None] * inv
    ang_c = col[:, None] * inv
    cos_r, sin_r, cos_c, sin_c = jnp.cos(ang_r), jnp.sin(ang_r), jnp.cos(ang_c), jnp.sin(ang_c)
    cos = jnp.concatenate([cos_r, cos_r, cos_c, cos_c], axis=1)
    sin = jnp.concatenate([-sin_r, sin_r, -sin_c, sin_c], axis=1)
    cos = jnp.concatenate([jnp.ones((CTX_LEN, HEAD_DIM), F32), cos], axis=0)
    sin = jnp.concatenate([jnp.zeros((CTX_LEN, HEAD_DIM), F32), sin], axis=0)
    return jnp.tile(cos, (1, 2)), jnp.tile(sin, (1, 2))


def _split_w_in(w_in):
    wb = w_in.astype(BF16)
    o_gr, o_ga, o_rq, o_rg = 1792, 2304, 2304 + 2 * GLA_RANK, 2304 + 2 * GLA_RANK + 1024
    pad = jnp.zeros(w_in.shape[:-1] + (LANES - 2 * GLA_RANK,), BF16)
    w1 = jnp.concatenate([wb[..., :o_gr], wb[..., o_rq:o_rg], wb[..., o_ga:o_rq], pad], axis=-1)
    w2 = jnp.concatenate([wb[..., o_gr:o_ga], wb[..., o_rg:]], axis=-1)
    return w1, w2


def kernel(x, c, ctx, c_ctx, ada_w, ada_b, norm1_g, norm2_g, w_in, attn_q_norm_g, attn_k_norm_g, attn_sink,
           gla_gate_w, gla_gate_b, gla_out_norm_g, ret_log_decay, ret_out_norm_g, w_branch, w_out, ffn_up,
           ffn_conv_w, ffn_conv_b, ffn_down):
    c_rows = jnp.concatenate([c, c_ctx[None], jnp.zeros((SUBLANES - BATCH - 1, D_MODEL), F32)], axis=0)
    mods = _ada_mods(c_rows, ada_w, ada_b).reshape(DEPTH, SUBLANES, 6, D_MODEL)
    pick = np.array([r for b in range(BATCH) for r in (BATCH, b)])
    mods = mods[:, pick]

    w1, w2 = _split_w_in(w_in)
    wbr = w_branch.astype(BF16)
    wo = w_out.astype(BF16)
    wup = ffn_up.astype(BF16)
    wdn = ffn_down.astype(BF16)
    cos_t, sin_t = _rope_tables()
    qg = jnp.tile(attn_q_norm_g, (1, 2)).reshape(DEPTH, 1, LANES)
    kg = jnp.tile(attn_k_norm_g, (1, 2)).reshape(DEPTH, 1, LANES)
    lane_group = np.arange(LANES) // HEAD_DIM
    mavg = jnp.asarray((lane_group[:, None] == lane_group[None, :]) / HEAD_DIM, BF16)
    wg = jnp.zeros((DEPTH, LANES, 512), F32)
    wg = wg.at[:, 0:GLA_RANK, 0:256].set(gla_gate_w[:, 0])
    wg = wg.at[:, GLA_RANK:2 * GLA_RANK, 256:512].set(gla_gate_w[:, 1]).astype(BF16)
    gbias = gla_gate_b.reshape(DEPTH, 1, 512)
    ld = jnp.repeat(ret_log_decay, RET_DK, axis=-1)
    rr = np.arange(SBLK)
    same = (rr[:, None] // CHUNK) == (rr[None, :] // CHUNK)
    tri_f = jnp.asarray(same & (rr[None, :] <= rr[:, None]), BF16)
    tri_b = jnp.asarray(same & (rr[None, :] >= rr[:, None]), BF16)

    n1g = norm1_g.reshape(DEPTH, 1, D_MODEL)
    n2g = norm2_g.reshape(DEPTH, 1, D_MODEL)
    gng = gla_out_norm_g.reshape(DEPTH, 1, LANES)
    rng = ret_out_norm_g.reshape(DEPTH, 1, LANES)
    cb = ffn_conv_b.reshape(DEPTH, 1, 2 * D_FF)

    xs = (x.reshape(BATCH * SEQ, D_MODEL), ctx.reshape(BATCH * CTX_LEN, D_MODEL))
    for l in range(DEPTH):
        qa, ka, va, gq, gk, gv, glog, rq, rk, rv = _inproj(
            l, xs, mods, n1g, w1, cos_t, sin_t, qg, kg, mavg, wg, gbias)
        by_batch = lambda a: a.reshape(BATCH, ROWS_B, a.shape[-1])
        flat = lambda a: a.reshape(NT, a.shape[-1])
        oa = _attention(l, attn_sink, by_batch(qa), by_batch(ka), va)
        gf, gb, rf, rb = _scans(l, by_batch(gq), by_batch(gk), by_batch(gv), by_batch(glog), tri_f, tri_b,
                                by_batch(rq), by_batch(rk), by_batch(rv), ld)
        xc = _merge(l, xs, mods, n1g, w2, oa, flat(gf), flat(gb), flat(rf), flat(rb), gng, rng, wbr, wo)
        xc = _ffn(l, xc, mods, n2g, wup, ffn_conv_w, cb, wdn, latent_only=(l == DEPTH - 1))
        xs = (xc,)
    return xc.reshape(BATCH, SEQ, D_MODEL)
```

```python
import functools

import jax
import jax.numpy as jnp
import numpy as np
from jax import lax
from jax.experimental import pallas as pl
from jax.experimental.pallas import tpu as pltpu

F32 = jnp.float32
BF16 = jnp.bfloat16

D_MODEL = 1024
BATCH = 2
SEQ = 8192
DEPTH = 4
CTX_LEN = 256
GRID_W = 64
EPS = 1e-6

ATT_HEADS = 8
ATT_KV_HEADS = 2
ATT_GROUP = ATT_HEADS // ATT_KV_HEADS
HEAD_DIM = 64
WBLK = 128
ROPE_BASE = 10000.0
ROPE_AXIS_DIM = HEAD_DIM // 2
ROPE_FREQS = ROPE_AXIS_DIM // 2

GLA_HEADS = 4
GLA_DK = 64
GLA_DV = 128
GLA_RANK = 16
GLA_TAU = 16.0
RET_HEADS = 4
RET_DK = 64
RET_DV = 128
CHUNK = 64
N_BRANCH = 3
BRANCH_W = 512
D_FF = 2816
LOG2E = 1.4426950408889634

LANES = 128
SUBLANES = 8
VMEM_LIMIT = 56 * 1024 * 1024

ROWS_B = CTX_LEN + SEQ
NT = BATCH * ROWS_B
TM = 256
NSUB = 3
TILES_B = ROWS_B // TM
N_TILES = NT // TM
ABLK_B = ROWS_B // WBLK
CTX_ABLK = CTX_LEN // WBLK
SBLK = 256
CH_PER_BLK = SBLK // CHUNK
FT = 256
HALO = SUBLANES

P1_W = 512 + 128 + 128 + 256 + 256 + 512 + 256 + 256 + 512 + LANES
P2_W = 512 + 512 + N_BRANCH * D_MODEL


def _const_spec(shape):
    nd = len(shape)
    return pl.BlockSpec(shape, lambda *_: (0,) * nd, pipeline_mode=pl.Buffered(1))


def _layer_spec(shape, l):
    nd = len(shape)
    return pl.BlockSpec((None,) + tuple(shape), lambda *_: (l,) + (0,) * nd, pipeline_mode=pl.Buffered(1))


def _params(sem):
    return pltpu.CompilerParams(dimension_semantics=sem, vmem_limit_bytes=VMEM_LIMIT)


def _sigmoid(x):
    return 1.0 / (1.0 + jnp.exp(-x))


def _silu(x):
    return x * _sigmoid(x)


def _mod_norm(x, g, sc, sh):
    ms = jnp.mean(x * x, axis=-1, keepdims=True)
    return (x * lax.rsqrt(ms + EPS) * g) * (1.0 + sc) + sh


def _mod_index(i):
    return 2 * (i // TILES_B) + jnp.minimum(i % TILES_B, 1)


def _ada_kernel(c_ref, w_ref, b_ref, o_ref):
    a = _silu(c_ref[...]).astype(BF16)
    o_ref[0] = jnp.dot(a, w_ref[0].astype(BF16), preferred_element_type=F32) + b_ref[0]


def _ada_mods(c_rows, ada_w, ada_b):
    width = 2 * D_MODEL
    return pl.pallas_call(
        _ada_kernel,
        grid=(DEPTH, 6 * D_MODEL // width),
        in_specs=[
            pl.BlockSpec((SUBLANES, D_MODEL), lambda l, j: (0, 0)),
            pl.BlockSpec((1, D_MODEL, width), lambda l, j: (l, 0, j)),
            pl.BlockSpec((1, 1, width), lambda l, j: (l, 0, j)),
        ],
        out_specs=pl.BlockSpec((1, SUBLANES, width), lambda l, j: (l, 0, j)),
        out_shape=jax.ShapeDtypeStruct((DEPTH, SUBLANES, 6 * D_MODEL), F32),
        compiler_params=_params(("arbitrary", "arbitrary")),
        name="ada_mods",
    )(c_rows, ada_w, ada_b.reshape(DEPTH, 1, 6 * D_MODEL))


def _head_rms(x, mavg):
    x2 = x * x
    hi = x2.astype(BF16)
    lo = (x2 - hi.astype(F32)).astype(BF16)
    return jnp.dot(hi, mavg, preferred_element_type=F32) + jnp.dot(lo, mavg, preferred_element_type=F32)


def _rope(x, cos, sin, first_half):
    up = pltpu.roll(x, LANES - ROPE_FREQS, 1)
    dn = pltpu.roll(x, ROPE_FREQS, 1)
    return x * cos + jnp.where(first_half, up, dn) * sin


def _row_tiles(split, refs):
    if not split:
        return [refs[0][t * TM:(t + 1) * TM, :] for t in range(NSUB)], refs[1:]
    is_ctx = pl.program_id(0) % (TILES_B // NSUB) == 0
    tiles = [r[...] for r in refs[:NSUB]]
    tiles[0] = jnp.where(is_ctx, refs[NSUB][...], tiles[0])
    return tiles, refs[NSUB + 1:]


def _row_tile_specs(split):
    if not split:
        return [pl.BlockSpec((NSUB * TM, D_MODEL), lambda s: (s, 0))]
    per_b = TILES_B // NSUB

    def latent(s, t):
        return ((s // per_b) * (TILES_B - 1) + jnp.maximum(NSUB * (s % per_b) + t - 1, 0), 0)

    return ([pl.BlockSpec((TM, D_MODEL), functools.partial(latent, t=t)) for t in range(NSUB)]
            + [pl.BlockSpec((TM, D_MODEL), lambda s: (s // per_b, 0))])


def _inproj_kernel(split, *refs):
    x_tiles, refs = _row_tiles(split, refs)
    mod_refs, refs = refs[:NSUB], refs[NSUB:]
    n1_ref, w_ref = refs[:2]
    cos_refs, sin_refs = refs[2:2 + NSUB], refs[2 + NSUB:2 + 2 * NSUB]
    qg_ref, kg_ref, mavg_ref, wg_ref, gb_ref = refs[2 + 2 * NSUB:7 + 2 * NSUB]
    qa_ref, ka_ref, va_ref, gq_ref, gk_ref, gv_ref, gl_ref, rq_ref, rk_ref, rv_ref = refs[7 + 2 * NSUB:]
    mavg = mavg_ref[...]
    lane = lax.broadcasted_iota(jnp.int32, (TM, LANES), 1)
    first_half = (lane % ROPE_AXIS_DIM) < ROPE_FREQS
    hbs = []
    for t in range(NSUB):
        sh = mod_refs[t][0, 0:1, :]
        sc = mod_refs[t][0, 1:2, :]
        hbs.append(_mod_norm(x_tiles[t], n1_ref[...], sc, sh).astype(BF16))

    for t in range(NSUB):
        hb = hbs[t]
        rows = slice(t * TM, (t + 1) * TM)
        cos = cos_refs[t][...]
        sin = sin_refs[t][...]

        def proj(lo, hi):
            return jnp.dot(hb, w_ref[:, lo:hi], preferred_element_type=F32)

        ze = proj(2816, 2944)
        za = proj(0, 768)
        gl = jnp.dot(ze.astype(BF16), wg_ref[...], preferred_element_type=F32) + gb_ref[...]
        zc = proj(1792, 2816)
        gl_ref[rows, :] = -(jnp.maximum(-gl, 0.0) + jnp.log1p(jnp.exp(-jnp.abs(gl)))) * (1.0 / GLA_TAU)
        qk = [za[:, cblk * LANES:(cblk + 1) * LANES] for cblk in range(5)]
        ms = [_head_rms(u, mavg) for u in qk]
        zb = proj(768, 1792)

        for cblk in range(5):
            g = qg_ref[...] if cblk < 4 else kg_ref[...]
            u = _rope(qk[cblk] * lax.rsqrt(ms[cblk] + EPS) * g, cos, sin, first_half)
            if cblk < 4:
                qa_ref[rows, cblk * LANES:(cblk + 1) * LANES] = (u * (HEAD_DIM ** -0.5 * LOG2E)).astype(BF16)
            else:
                ka_ref[rows, :] = u.astype(BF16)
        va_ref[:, rows] = za[:, 640:768].T.astype(BF16)

        for cblk in range(2):
            sl = slice(cblk * LANES, (cblk + 1) * LANES)
            rq_ref[rows, sl] = _rope(zc[:, sl], cos, sin, first_half).astype(BF16)
            rk = zc[:, 256 + cblk * LANES:256 + (cblk + 1) * LANES] * RET_DK ** -0.5
            rk_ref[rows, sl] = _rope(rk, cos, sin, first_half).astype(BF16)
        rv_ref[rows, :] = zc[:, 512:1024].astype(BF16)

        gq_ref[rows, :] = (zb[:, 0:256] * GLA_DK ** -0.5).astype(BF16)
        gk_ref[rows, :] = zb[:, 256:512].astype(BF16)
        gv_ref[rows, :] = zb[:, 512:1024].astype(BF16)


def _mod_spec(l):
    return pl.BlockSpec((None, 1, 6, D_MODEL), lambda i: (l, _mod_index(i), 0, 0))


def _mod_specs(l, nsub):
    return [pl.BlockSpec((None, 1, 6, D_MODEL), functools.partial(
        lambda s, t: (l, _mod_index(nsub * s + t), 0, 0), t=t)) for t in range(nsub)]


def _inproj(l, xs, mod, n1g, w1, cos_t, sin_t, qg, kg, mavg, wg, gbias):
    split = len(xs) == 2
    stream = [xs[0]] * NSUB + [xs[1]] if split else list(xs)
    row = lambda w: pl.BlockSpec((NSUB * TM, w), lambda s: (s, 0))
    table = [pl.BlockSpec((TM, LANES), functools.partial(lambda s, t: ((NSUB * s + t) % TILES_B, 0), t=t))
             for t in range(NSUB)]
    widths = (512, 128, 128, 256, 256, 512, 512, 256, 256, 512)
    dtypes = (BF16, BF16, BF16, BF16, BF16, BF16, F32, BF16, BF16, BF16)
    return pl.pallas_call(
        functools.partial(_inproj_kernel, split),
        grid=(N_TILES // NSUB,),
        in_specs=_row_tile_specs(split) + _mod_specs(l, NSUB) + [
            _layer_spec((1, D_MODEL), l),
            _layer_spec((D_MODEL, P1_W), l),
        ] + table + table + [
            _layer_spec((1, LANES), l),
            _layer_spec((1, LANES), l),
            _const_spec((LANES, LANES)),
            _layer_spec((LANES, 512), l),
            _layer_spec((1, 512), l),
        ],
        out_specs=[pl.BlockSpec((None, LANES, NSUB * TM),
                                lambda s: (s // (TILES_B // NSUB), 0, s % (TILES_B // NSUB)))
                   if n == 2 else row(w) for n, w in enumerate(widths)],
        out_shape=[jax.ShapeDtypeStruct((BATCH, LANES, ROWS_B) if n == 2 else (NT, w), d)
                   for n, (w, d) in enumerate(zip(widths, dtypes))],
        compiler_params=_params(("arbitrary",)),
        name="in_proj",
    )(*stream, *([mod] * NSUB), n1g, w1, *([cos_t] * NSUB), *([sin_t] * NSUB), qg, kg, mavg, wg, gbias)


def _attn_kernel(l, sink_ref, band_ref, q_ref, kl_ref, ks_ref, kr_ref, kc_ref, *refs):
    v_refs, o_ref = refs[:-1], refs[-1]
    r = pl.program_id(0)
    n_loc = 3 * WBLK
    n_key = n_loc + CTX_LEN
    n_q = ATT_GROUP * WBLK
    kk = lax.broadcasted_iota(jnp.int32, (n_key, 1), 0)
    kblk = r - 1 + kk // WBLK
    blk_ok = (kk >= n_loc) | ((kblk >= CTX_ABLK) & (kblk <= ABLK_B - 1) & (r >= CTX_ABLK))
    bias1 = band_ref[...] + jnp.where(blk_ok, 0.0, -jnp.inf)
    bias = jnp.concatenate([bias1] * ATT_GROUP, axis=1)
    head_of_lane = lax.broadcasted_iota(jnp.int32, (1, n_q), 1) // WBLK

    chains = [(b, kh) for b in range(BATCH) for kh in range(ATT_KV_HEADS)]
    k_all = [jnp.concatenate([kl_ref[b], ks_ref[b], kr_ref[b], kc_ref[b]], axis=0) for b in range(BATCH)]
    vt_all = [jnp.concatenate([v[b] for v in v_refs], axis=1) for b in range(BATCH)]
    scores = []
    for b, kh in chains:
        q = q_ref[b]
        q_s = jnp.concatenate([q[:, h * HEAD_DIM:(h + 1) * HEAD_DIM]
                               for h in range(kh * ATT_GROUP, (kh + 1) * ATT_GROUP)], axis=0)
        k_h = k_all[b][:, kh * HEAD_DIM:(kh + 1) * HEAD_DIM]
        scores.append(lax.dot_general(k_h, q_s, (((1,), (1,)), ((), ())), preferred_element_type=F32))
    for (b, kh), sc in zip(chains, scores):
        vt_h = vt_all[b][kh * HEAD_DIM:(kh + 1) * HEAD_DIM, :]
        sink = jnp.zeros((1, n_q), F32)
        for g in range(ATT_GROUP):
            sink = jnp.where(head_of_lane == g, sink_ref[l, kh * ATT_GROUP + g] * LOG2E, sink)
        s = sc + bias
        m = jnp.maximum(jnp.max(s, axis=0, keepdims=True), sink)
        e = jnp.exp2(s - m)
        denom = jnp.sum(e, axis=0, keepdims=True) + jnp.exp2(sink - m)
        o = jnp.dot(vt_h, e.astype(BF16), preferred_element_type=F32) * (1.0 / denom)
        for g in range(ATT_GROUP):
            h = kh * ATT_GROUP + g
            o_ref[b, h * HEAD_DIM:(h + 1) * HEAD_DIM, :] = o[:, g * WBLK:(g + 1) * WBLK].astype(BF16)


def _band_bias():
    kk = np.arange(3 * WBLK + CTX_LEN)[:, None]
    qq = np.arange(WBLK)[None, :]
    ok = (kk >= 3 * WBLK) | (np.abs(kk - WBLK - qq) <= WBLK)
    return jnp.asarray(np.where(ok, 0.0, -np.inf), F32)


def _attention(l, sink, qa, ka, va_t):
    blk = lambda f: pl.BlockSpec((BATCH, WBLK, LANES), f)
    left = lambda r: (0, jnp.maximum(r - 1, 0), 0)
    here = lambda r: (0, r, 0)
    right = lambda r: (0, jnp.minimum(r + 1, ABLK_B - 1), 0)
    ctx = pl.BlockSpec((BATCH, CTX_LEN, LANES), lambda r: (0, 0, 0))
    col = lambda f: (lambda r: (0, 0, f(r)[1]))
    v_specs = [pl.BlockSpec((BATCH, LANES, WBLK), col(left)), pl.BlockSpec((BATCH, LANES, WBLK), col(here)),
               pl.BlockSpec((BATCH, LANES, WBLK), col(right)),
               pl.BlockSpec((BATCH, LANES, CTX_LEN), lambda r: (0, 0, 0))]
    return pl.pallas_call(
        functools.partial(_attn_kernel, l),
        grid=(ABLK_B,),
        in_specs=[
            pl.BlockSpec(memory_space=pltpu.SMEM),
            _const_spec((3 * WBLK + CTX_LEN, WBLK)),
            pl.BlockSpec((BATCH, WBLK, 512), here),
            blk(left), blk(here), blk(right), ctx,
        ] + v_specs,
        out_specs=pl.BlockSpec((BATCH, ATT_HEADS * HEAD_DIM, WBLK), lambda r: (0, 0, r)),
        out_shape=jax.ShapeDtypeStruct((BATCH, ATT_HEADS * HEAD_DIM, ROWS_B), BF16),
        compiler_params=_params(("arbitrary",)),
        name="window_attn",
    )(sink, _band_bias(), qa, ka, ka, ka, ka, va_t, va_t, va_t, va_t)


def _split3(g):
    hi = g.astype(BF16)
    r1 = g - hi.astype(F32)
    mid = r1.astype(BF16)
    lo = (r1 - mid.astype(F32)).astype(BF16)
    return hi, mid, lo


_K_SL = [slice(h * GLA_DK, (h + 1) * GLA_DK) for h in range(GLA_HEADS)]
_V_SL = [slice(h * GLA_DV, (h + 1) * GLA_DV) for h in range(GLA_HEADS)]
_R_SL = [slice(c * CHUNK, (c + 1) * CHUNK) for c in range(CH_PER_BLK)]


def _scan_prep(q, k, b, tot_rows):
    width = GLA_HEADS * GLA_DK
    tot = jnp.concatenate([jnp.broadcast_to(t, (CHUNK, width)) for t in tot_rows], axis=0)
    q = q.astype(F32)
    k = k.astype(F32)
    qt = (q * jnp.exp(b)).astype(BF16)
    kt = (k * jnp.exp(-b)).astype(BF16)
    kend_t = (k * jnp.exp(tot - b)).T.astype(BF16)
    row_id = lax.broadcasted_iota(jnp.int32, (LANES, width), 0)
    tot_mat = jnp.zeros((LANES, width), F32)
    for c, t in enumerate(tot_rows):
        tot_mat = jnp.where(row_id == c, t, tot_mat)
    dec_t = jnp.exp(tot_mat.T)
    return qt, kt, kend_t, dec_t


def _scan_local(prep, v, reverse):
    qt, kt, kend_t, _ = prep
    ii = lax.broadcasted_iota(jnp.int32, (SBLK, SBLK), 0)
    jj = lax.broadcasted_iota(jnp.int32, (SBLK, SBLK), 1)
    tri = jnp.logical_and((ii // CHUNK) == (jj // CHUNK), (jj >= ii) if reverse else (jj <= ii))
    att = [lax.dot_general(qt[:, ks], kt[:, ks], (((1,), (1,)), ((), ())), preferred_element_type=F32)
           for ks in _K_SL]
    ds = [[jnp.dot(kend_t[ks, rows], v[rows, vs], preferred_element_type=F32)
           for ks, vs in zip(_K_SL, _V_SL)] for rows in _R_SL]
    intra = [jnp.dot(jnp.where(tri, a, 0.0).astype(BF16), v[:, vs], preferred_element_type=F32)
             for a, vs in zip(att, _V_SL)]
    return ds, intra


def _scan_finish(prep, local, s_ref, o_ref, reverse):
    qt, _, _, dec_t = prep
    ds, intra = local
    order = range(CH_PER_BLK - 1, -1, -1) if reverse else range(CH_PER_BLK)
    state = [s_ref[ks, :] for ks in _K_SL]
    start = [None] * CH_PER_BLK
    for c in order:
        start[c] = [s.astype(BF16) for s in state]
        state = [s * dec_t[ks, c:c + 1] + d for s, ks, d in zip(state, _K_SL, ds[c])]
    for ks, s in zip(_K_SL, state):
        s_ref[ks, :] = s
    for c, rows in enumerate(_R_SL):
        for h in range(GLA_HEADS):
            inter = jnp.dot(qt[rows, _K_SL[h]], start[c][h], preferred_element_type=F32)
            o_ref[rows, _V_SL[h]] = intra[h][rows, :] + inter


def _run_chains(chains):
    preps = [_scan_prep(q, k, b, tot) for q, k, _, b, tot, _, _, _ in chains]
    locs = [_scan_local(p, ch[2], ch[7]) for p, ch in zip(preps, chains)]
    for p, loc, ch in zip(preps, locs, chains):
        _scan_finish(p, loc, ch[5], ch[6], ch[7])


def _scan_kernel(gqf_ref, gkf_ref, gvf_ref, ggf_ref, gqb_ref, gkb_ref, gvb_ref, ggb_ref, lf_ref, lb_ref,
                 rqf_ref, rkf_ref, rvf_ref, rqb_ref, rkb_ref, rvb_ref, ld_ref,
                 gof_ref, gob_ref, rof_ref, rob_ref, gsf_ref, gsb_ref, rsf_ref, rsb_ref):
    @pl.when(pl.program_id(0) == 0)
    def _():
        for s_ref in (gsf_ref, gsb_ref, rsf_ref, rsb_ref):
            s_ref[...] = jnp.zeros_like(s_ref)

    def cum(g, tri_ref):
        tri = tri_ref[...]
        return sum(jnp.dot(tri, part, preferred_element_type=F32) for part in _split3(g))

    pos = (lax.broadcasted_iota(jnp.int32, (SBLK, RET_HEADS * RET_DK), 0) % CHUNK).astype(F32)
    lg_f = -jnp.exp(ld_ref[0:1, :])
    lg_b = -jnp.exp(ld_ref[1:2, :])
    b_f = (pos + 1.0) * lg_f
    b_b = (CHUNK - pos) * lg_b
    chains = []
    for bi in range(BATCH):
        bf = cum(ggf_ref[bi], lf_ref)
        tot_f = [bf[(c + 1) * CHUNK - 1:(c + 1) * CHUNK, :] for c in range(CH_PER_BLK)]
        chains.append((gqf_ref[bi], gkf_ref[bi], gvf_ref[bi], bf, tot_f, gsf_ref.at[bi], gof_ref.at[bi], False))
        bb = cum(ggb_ref[bi], lb_ref)
        tot_b = [bb[c * CHUNK:c * CHUNK + 1, :] for c in range(CH_PER_BLK)]
        chains.append((gqb_ref[bi], gkb_ref[bi], gvb_ref[bi], bb, tot_b, gsb_ref.at[bi], gob_ref.at[bi], True))
        chains.append((rqf_ref[bi], rkf_ref[bi], rvf_ref[bi], b_f, [lg_f * CHUNK] * CH_PER_BLK,
                       rsf_ref.at[bi], rof_ref.at[bi], False))
        chains.append((rqb_ref[bi], rkb_ref[bi], rvb_ref[bi], b_b, [lg_b * CHUNK] * CH_PER_BLK,
                       rsb_ref.at[bi], rob_ref.at[bi], True))
    _run_chains(chains)


def _bwd_block(s):
    return jnp.where(s == 0, 0, TILES_B - s)


def _scan_specs(width):
    fwd = pl.BlockSpec((BATCH, SBLK, width), lambda s: (0, s, 0))
    bwd = pl.BlockSpec((BATCH, SBLK, width), lambda s: (0, _bwd_block(s), 0))
    return fwd, bwd


def _scans(l, gq, gk, gv, glog, tri_f, tri_b, rq, rk, rv, ld):
    qf, qb = _scan_specs(256)
    vf, vb = _scan_specs(512)
    gf = pl.BlockSpec((BATCH, SBLK, 256), lambda s: (0, s, 0))
    gb = pl.BlockSpec((BATCH, SBLK, 256), lambda s: (0, _bwd_block(s), 1))
    tri = _const_spec((SBLK, SBLK))
    return pl.pallas_call(
        _scan_kernel,
        grid=(TILES_B,),
        in_specs=[qf, qf, vf, gf, qb, qb, vb, gb, tri, tri, qf, qf, vf, qb, qb, vb, _layer_spec((2, 256), l)],
        out_specs=[vf, vb, vf, vb],
        out_shape=[jax.ShapeDtypeStruct((BATCH, ROWS_B, 512), F32)] * 4,
        scratch_shapes=[pltpu.VMEM((BATCH, GLA_HEADS * GLA_DK, GLA_DV), F32)] * 4,
        compiler_params=_params(("arbitrary",)),
        name="bidir_scans",
    )(gq, gk, gv, glog, gq, gk, gv, glog, tri_f, tri_b, rq, rk, rv, rq, rk, rv, ld)


def _merge_kernel(split, *refs):
    x_tiles, refs = _row_tiles(split, refs)
    mod_refs, oa_refs = refs[:NSUB], refs[NSUB:2 * NSUB]
    n1_ref, w2_ref, gf_ref, gb_ref, rf_ref, rb_ref, gn_ref, rn_ref, wbr_ref, wo_ref, y_ref = refs[2 * NSUB:]

    def gated(o, gate, ng):
        parts = []
        for h in range(4):
            sl = slice(h * LANES, (h + 1) * LANES)
            oh = o[:, sl]
            ms = jnp.mean(oh * oh, axis=-1, keepdims=True)
            parts.append((oh * lax.rsqrt(ms + EPS) * ng) * _silu(gate[:, sl]))
        return jnp.concatenate(parts, axis=1).astype(BF16)

    hbs = []
    for t in range(NSUB):
        sh = mod_refs[t][0, 0:1, :]
        sc = mod_refs[t][0, 1:2, :]
        hbs.append(_mod_norm(x_tiles[t], n1_ref[...], sc, sh).astype(BF16))

    for t in range(NSUB):
        hb = hbs[t]
        rows = slice(t * TM, (t + 1) * TM)
        gates = jnp.dot(hb, w2_ref[:, 0:1024], preferred_element_type=F32)
        mg = [jnp.dot(hb, w2_ref[:, 1024 + z * D_MODEL:1024 + (z + 1) * D_MODEL], preferred_element_type=F32)
              for z in range(N_BRANCH)]
        ys0 = lax.dot_general(oa_refs[t][...], wbr_ref[0], (((0,), (0,)), ((), ())), preferred_element_type=F32)
        branches = (
            None,
            gated(gf_ref[rows, :] + gb_ref[rows, :], gates[:, 0:512], gn_ref[...]),
            gated(rf_ref[rows, :] + rb_ref[rows, :], gates[:, 512:1024], rn_ref[...]),
        )
        acc = _sigmoid(mg[0]) * ys0
        for z in range(1, N_BRANCH):
            acc = acc + _sigmoid(mg[z]) * jnp.dot(branches[z], wbr_ref[z], preferred_element_type=F32)
        y = jnp.dot(acc.astype(BF16), wo_ref[...], preferred_element_type=F32)
        y_ref[rows, :] = x_tiles[t] + mod_refs[t][0, 2:3, :] * y


def _merge(l, xs, mod, n1g, w2, oa, gf, gb, rf, rb, gng, rng, wbr, wo):
    split = len(xs) == 2
    stream = [xs[0]] * NSUB + [xs[1]] if split else list(xs)
    row = lambda w: pl.BlockSpec((NSUB * TM, w), lambda s: (s, 0))
    oa_specs = [pl.BlockSpec((None, ATT_HEADS * HEAD_DIM, TM), functools.partial(
        lambda s, t: ((NSUB * s + t) // TILES_B, 0, (NSUB * s + t) % TILES_B), t=t)) for t in range(NSUB)]
    return pl.pallas_call(
        functools.partial(_merge_kernel, split),
        grid=(N_TILES // NSUB,),
        in_specs=_row_tile_specs(split) + _mod_specs(l, NSUB) + oa_specs + [
            _layer_spec((1, D_MODEL), l),
            _layer_spec((D_MODEL, P2_W), l),
            row(512), row(512), row(512), row(512),
            _layer_spec((1, LANES), l),
            _layer_spec((1, LANES), l),
            _layer_spec((N_BRANCH, BRANCH_W, D_MODEL), l),
            _layer_spec((D_MODEL, D_MODEL), l),
        ],
        out_specs=row(D_MODEL),
        out_shape=jax.ShapeDtypeStruct((NT, D_MODEL), F32),
        compiler_params=_params(("arbitrary",)),
        name="merge_out",
    )(*stream, *([mod] * NSUB), *([oa] * NSUB), n1g, w2, gf, gb, rf, rb, gng, rng, wbr, wo)


def _ffn_kernel(nsub, x_ref, xp_ref, xn_ref, *refs):
    mod_refs = refs[:nsub]
    n2_ref, wup_ref, cw_ref, cb_ref, wdn_ref, y_ref, g_ref = refs[nsub:]
    g2 = n2_ref[...]
    hs = []
    for t in range(nsub):
        j = (pl.program_id(0) * nsub + t) % TILES_B
        sh = mod_refs[t][0, 3:4, :]
        sc = mod_refs[t][0, 4:5, :]
        prev_ok = j >= 2
        next_ok = jnp.logical_and(j >= 1, j <= TILES_B - 2)
        xp = xp_ref[...] if t == 0 else x_ref[t * TM - HALO:t * TM, :]
        xn = xn_ref[...] if t == nsub - 1 else x_ref[(t + 1) * TM:(t + 1) * TM + HALO, :]
        hp = jnp.where(prev_ok, _mod_norm(xp, g2, sc, sh), 0.0)
        hn = jnp.where(next_ok, _mod_norm(xn, g2, sc, sh), 0.0)
        hm = _mod_norm(x_ref[t * TM:(t + 1) * TM, :], g2, sc, sh)
        hs.append(jnp.concatenate([hp, hm, hn], axis=0).astype(BF16))

    def conv(u, col):
        w = cw_ref[:, col:col + FT]
        prev = pltpu.roll(u, 1, 0)[HALO:HALO + TM]
        nxt = pltpu.roll(u, TM + 2 * HALO - 1, 0)[HALO:HALO + TM]
        return prev * w[0:1] + u[HALO:HALO + TM] * w[1:2] + nxt * w[2:3] + cb_ref[:, col:col + FT]

    for t in range(nsub):
        h = hs[t]
        for f in range(D_FF // FT):
            a = conv(jnp.dot(h, wup_ref[:, f * FT:(f + 1) * FT], preferred_element_type=F32), f * FT)
            bv = conv(jnp.dot(h, wup_ref[:, D_FF + f * FT:D_FF + (f + 1) * FT], preferred_element_type=F32),
                      D_FF + f * FT)
            g_ref[t, :, f * FT:(f + 1) * FT] = (_silu(a) * bv).astype(BF16)
        gt = mod_refs[t][0, 5:6, :]
        rows = slice(t * TM, (t + 1) * TM)
        y_ref[rows, :] = x_ref[rows, :] + gt * jnp.dot(g_ref[t], wdn_ref[...], preferred_element_type=F32)


def _ffn(l, xc, mod, n2g, wup, cw, cb, wdn, latent_only=False):
    per_tile = TM // HALO
    if latent_only:
        nsub = 1
        out_rows = BATCH * SEQ
        out_map = lambda i: ((i // TILES_B) * (TILES_B - 1) + jnp.maximum(i % TILES_B - 1, 0), 0)
    else:
        nsub = NSUB
        out_rows = NT
        out_map = lambda s: (s, 0)
    return pl.pallas_call(
        functools.partial(_ffn_kernel, nsub),
        grid=(N_TILES // nsub,),
        in_specs=[
            pl.BlockSpec((nsub * TM, D_MODEL), lambda s: (s, 0)),
            pl.BlockSpec((HALO, D_MODEL), lambda s: (jnp.maximum(s * nsub * per_tile - 1, 0), 0)),
            pl.BlockSpec((HALO, D_MODEL),
                         lambda s: (jnp.minimum((s + 1) * nsub * per_tile, NT // HALO - 1), 0)),
        ] + _mod_specs(l, nsub) + [
            _layer_spec((1, D_MODEL), l),
            _layer_spec((D_MODEL, 2 * D_FF), l),
            _layer_spec((3, 2 * D_FF), l),
            _layer_spec((1, 2 * D_FF), l),
            _layer_spec((D_FF, D_MODEL), l),
        ],
        out_specs=pl.BlockSpec((nsub * TM, D_MODEL), out_map),
        out_shape=jax.ShapeDtypeStruct((out_rows, D_MODEL), F32),
        scratch_shapes=[pltpu.VMEM((nsub, TM, D_FF), BF16)],
        compiler_params=_params(("arbitrary",)),
        name="conv_ffn",
    )(xc, xc, xc, *([mod] * nsub), n2g, wup, cw, cb, wdn)


def _rope_tables():
    t = jnp.arange(SEQ)
    row = (t // GRID_W).astype(F32)
    col = (t % GRID_W).astype(F32)
    inv = ROPE_BASE ** (-jnp.arange(ROPE_FREQS, dtype=F32) * 2.0 / ROPE_AXIS_DIM)
    ang_r = row[:, None] * inv
    ang_c = col[:, None] * inv
    cos_r, sin_r, cos_c, sin_c = jnp.cos(ang_r), jnp.sin(ang_r), jnp.cos(ang_c), jnp.sin(ang_c)
    cos = jnp.concatenate([cos_r, cos_r, cos_c, cos_c], axis=1)
    sin = jnp.concatenate([-sin_r, sin_r, -sin_c, sin_c], axis=1)
    cos = jnp.concatenate([jnp.ones((CTX_LEN, HEAD_DIM), F32), cos], axis=0)
    sin = jnp.concatenate([jnp.zeros((CTX_LEN, HEAD_DIM), F32), sin], axis=0)
    return jnp.tile(cos, (1, 2)), jnp.tile(sin, (1, 2))


W_IN_COLS = 2304 + 2 * GLA_RANK + 1024 + 512 + N_BRANCH * D_MODEL
WPREP_ROWS = 128


def _wprep_kernel(w_ref, w1_ref, w2_ref):
    o_gr, o_ga, o_rq, o_rg = 1792, 2304, 2304 + 2 * GLA_RANK, 2304 + 2 * GLA_RANK + 1024
    w = w_ref[0]
    pad = jnp.zeros((WPREP_ROWS, LANES - 2 * GLA_RANK), F32)
    w1_ref[0] = jnp.concatenate([w[:, :o_gr], w[:, o_rq:o_rg], w[:, o_ga:o_rq], pad], axis=1).astype(BF16)
    w2_ref[0] = jnp.concatenate([w[:, o_gr:o_ga], w[:, o_rg:]], axis=1).astype(BF16)


def _split_w_in(w_in):
    return pl.pallas_call(
        _wprep_kernel,
        grid=(DEPTH, D_MODEL // WPREP_ROWS),
        in_specs=[pl.BlockSpec((1, WPREP_ROWS, W_IN_COLS), lambda l, i: (l, i, 0))],
        out_specs=[pl.BlockSpec((1, WPREP_ROWS, P1_W), lambda l, i: (l, i, 0)),
                   pl.BlockSpec((1, WPREP_ROWS, P2_W), lambda l, i: (l, i, 0))],
        out_shape=[jax.ShapeDtypeStruct((DEPTH, D_MODEL, P1_W), BF16),
                   jax.ShapeDtypeStruct((DEPTH, D_MODEL, P2_W), BF16)],
        compiler_params=_params(("arbitrary", "arbitrary")),
        name="w_in_prep",
    )(w_in)


def kernel(x, c, ctx, c_ctx, ada_w, ada_b, norm1_g, norm2_g, w_in, attn_q_norm_g, attn_k_norm_g, attn_sink,
           gla_gate_w, gla_gate_b, gla_out_norm_g, ret_log_decay, ret_out_norm_g, w_branch, w_out, ffn_up,
           ffn_conv_w, ffn_conv_b, ffn_down):
    c_rows = jnp.concatenate([c, c_ctx[None], jnp.zeros((SUBLANES - BATCH - 1, D_MODEL), F32)], axis=0)
    mods = _ada_mods(c_rows, ada_w, ada_b).reshape(DEPTH, SUBLANES, 6, D_MODEL)
    pick = np.array([r for b in range(BATCH) for r in (BATCH, b)])
    mods = mods[:, pick]

    w1, w2 = _split_w_in(w_in)
    wbr = w_branch.astype(BF16)
    wo = w_out.astype(BF16)
    wup = ffn_up.astype(BF16)
    wdn = ffn_down.astype(BF16)
    cos_t, sin_t = _rope_tables()
    qg = jnp.tile(attn_q_norm_g, (1, 2)).reshape(DEPTH, 1, LANES)
    kg = jnp.tile(attn_k_norm_g, (1, 2)).reshape(DEPTH, 1, LANES)
    lane_group = np.arange(LANES) // HEAD_DIM
    mavg = jnp.asarray((lane_group[:, None] == lane_group[None, :]) / HEAD_DIM, BF16)
    wg = jnp.zeros((DEPTH, LANES, 512), F32)
    wg = wg.at[:, 0:GLA_RANK, 0:256].set(gla_gate_w[:, 0])
    wg = wg.at[:, GLA_RANK:2 * GLA_RANK, 256:512].set(gla_gate_w[:, 1]).astype(BF16)
    gbias = gla_gate_b.reshape(DEPTH, 1, 512)
    ld = jnp.repeat(ret_log_decay, RET_DK, axis=-1)
    rr = np.arange(SBLK)
    same = (rr[:, None] // CHUNK) == (rr[None, :] // CHUNK)
    tri_f = jnp.asarray(same & (rr[None, :] <= rr[:, None]), BF16)
    tri_b = jnp.asarray(same & (rr[None, :] >= rr[:, None]), BF16)

    n1g = norm1_g.reshape(DEPTH, 1, D_MODEL)
    n2g = norm2_g.reshape(DEPTH, 1, D_MODEL)
    gng = gla_out_norm_g.reshape(DEPTH, 1, LANES)
    rng = ret_out_norm_g.reshape(DEPTH, 1, LANES)
    cb = ffn_conv_b.reshape(DEPTH, 1, 2 * D_FF)

    xs = (x.reshape(BATCH * SEQ, D_MODEL), ctx.reshape(BATCH * CTX_LEN, D_MODEL))
    for l in range(DEPTH):
        qa, ka, va, gq, gk, gv, glog, rq, rk, rv = _inproj(
            l, xs, mods, n1g, w1, cos_t, sin_t, qg, kg, mavg, wg, gbias)
        by_batch = lambda a: a.reshape(BATCH, ROWS_B, a.shape[-1])
        flat = lambda a: a.reshape(NT, a.shape[-1])
        oa = _attention(l, attn_sink, by_batch(qa), by_batch(ka), va)
        gf, gb, rf, rb = _scans(l, by_batch(gq), by_batch(gk), by_batch(gv), by_batch(glog), tri_f, tri_b,
                                by_batch(rq), by_batch(rk), by_batch(rv), ld)
        xc = _merge(l, xs, mods, n1g, w2, oa, flat(gf), flat(gb), flat(rf), flat(rb), gng, rng, wbr, wo)
        xc = _ffn(l, xc, mods, n2g, wup, ffn_conv_w, cb, wdn, latent_only=(l == DEPTH - 1))
        xs = (xc,)
    return xc.reshape(BATCH, SEQ, D_MODEL)
```
